```python
import jax, jax.numpy as jnp
from jax import lax
import numpy as np

D_MODEL = 2048
BATCH = 2
SEQ = 8192
DEPTH = 1

CHUNK = 64
D_MIX = D_MODEL
D_SSM = D_MIX // 2
D_CONV = D_MIX - D_SSM
SSM_GROUP = 16
N_SSM_GROUPS = D_SSM // SSM_GROUP
SSM_STATE = 64
CONV_K = 31
D_FF = (((8 * D_MODEL) // 3 + 255) // 256) * 256
FFN_CONV_K = 3
EPS = 1e-6
DT_MIN = 1e-3
DT_MAX = 1e-1

kernel_name = "hybrid_s5_conformer_convffn_layer"


def rms_norm(x, gain):
    x32 = x.astype(jnp.float32)
    y = x32 * lax.rsqrt(jnp.mean(x32 * x32, axis=-1, keepdims=True) + EPS)
    return (y * gain.astype(jnp.float32)).astype(x.dtype)


def layer_norm(x, gain, bias):
    x32 = x.astype(jnp.float32)
    mu = jnp.mean(x32, axis=-1, keepdims=True)
    xc = x32 - mu
    y = xc * lax.rsqrt(jnp.mean(xc * xc, axis=-1, keepdims=True) + EPS)
    return (y * gain.astype(jnp.float32) + bias.astype(jnp.float32)).astype(x.dtype)


def causal_depthwise_conv(x, w):
    k, c = w.shape
    return lax.conv_general_dilated(
        x, w[:, None, :].astype(x.dtype), window_strides=(1,), padding=[(k - 1, 0)],
        dimension_numbers=("NWC", "WIO", "NWC"), feature_group_count=c)


def _linear_recurrence(e1, e2):
    a1, b1 = e1
    a2, b2 = e2
    return a1 * a2, a2 * b1 + b2


def s5_mixer(u, log_dt, lam_re, lam_im, b_re, b_im, c_re, c_im, d, w_glu):
    bsz, seq, _ = u.shape
    nc = seq // CHUNK
    u32 = u.astype(jnp.float32).reshape(bsz, seq, N_SSM_GROUPS, SSM_GROUP)
    lam = lax.complex(lam_re.astype(jnp.float32), lam_im.astype(jnp.float32))
    dt = jnp.exp(log_dt.astype(jnp.float32))[:, None]
    lam_dt = lam * dt
    lam_bar = jnp.exp(lam_dt)
    b = lax.complex(b_re.astype(jnp.float32), b_im.astype(jnp.float32))
    b_bar = ((lam_bar - 1.0) / lam)[..., None] * b
    c = lax.complex(c_re.astype(jnp.float32), c_im.astype(jnp.float32))
    bu = jnp.einsum("gph,blgh->blgp", b_bar, u32.astype(jnp.complex64))
    bu = bu.reshape(bsz, nc, CHUNK, N_SSM_GROUPS, SSM_STATE)
    a_local = jnp.broadcast_to(lam_bar, (1, 1, CHUNK, N_SSM_GROUPS, SSM_STATE))
    _, h_local = lax.associative_scan(_linear_recurrence, (a_local, bu), axis=2)
    steps = jnp.arange(1, CHUNK + 1, dtype=jnp.float32)[:, None, None]
    lam_pow = jnp.exp(lam_dt[None] * steps)
    a_chunk = jnp.broadcast_to(lam_pow[-1], (1, nc, N_SSM_GROUPS, SSM_STATE))
    _, s_end = lax.associative_scan(_linear_recurrence, (a_chunk, h_local[:, :, -1]), axis=1)
    s_prev = jnp.concatenate([jnp.zeros_like(s_end[:, :1]), s_end[:, :-1]], axis=1)
    h = h_local + lam_pow[None, None] * s_prev[:, :, None]
    h = h.reshape(bsz, seq, N_SSM_GROUPS, SSM_STATE)
    y = jnp.einsum("ghp,blgp->blgh", c, h).real + d.astype(jnp.float32).reshape(N_SSM_GROUPS, SSM_GROUP) * u32
    y = jax.nn.gelu(y.reshape(bsz, seq, D_SSM))
    y = y * jax.nn.sigmoid(y @ w_glu.astype(jnp.float32))
    return y.astype(u.dtype)


def conformer_conv_mixer(v, g, conv_w, ln_g, ln_b):
    z = v * jax.nn.sigmoid(g)
    z = causal_depthwise_conv(z, conv_w)
    z = layer_norm(z, ln_g, ln_b)
    return jax.nn.silu(z)


def conv_ffn(h, w_up, conv_w, w_down):
    up = causal_depthwise_conv(h @ w_up, conv_w)
    gate, val = jnp.split(up, 2, axis=-1)
    return (jax.nn.gelu(gate) * val) @ w_down


def setup_inputs(seed: int = 0) -> dict:
    key = jax.random.key(seed)
    ks = jax.random.split(key, 24)
    f32 = jnp.float32
    nrm = lambda k, shape, scale: jax.random.normal(k, shape, f32) * scale
    gain = lambda k, n: 1.0 + 0.01 * jax.random.normal(k, (DEPTH, n), f32)
    n_idx = jnp.arange(SSM_STATE, dtype=f32)
    lam_re = -0.5 + 0.01 * jax.random.normal(ks[3], (DEPTH, N_SSM_GROUPS, SSM_STATE), f32)
    lam_im = jnp.pi * n_idx + 0.01 * jax.random.normal(ks[4], (DEPTH, N_SSM_GROUPS, SSM_STATE), f32)
    log_dt = jax.random.uniform(ks[5], (DEPTH, N_SSM_GROUPS), f32, np.log(DT_MIN), np.log(DT_MAX))
    return {
        "x": nrm(ks[0], (BATCH, SEQ, D_MODEL), 1.0),
        "pre_mix_g": gain(ks[1], D_MODEL),
        "w_in": nrm(ks[2], (DEPTH, D_MODEL, D_SSM + 2 * D_CONV), D_MODEL ** -0.5),
        "ssm_log_dt": log_dt,
        "ssm_lam_re": lam_re,
        "ssm_lam_im": lam_im,
        "ssm_b_re": nrm(ks[6], (DEPTH, N_SSM_GROUPS, SSM_STATE, SSM_GROUP), (2 * SSM_GROUP) ** -0.5),
        "ssm_b_im": nrm(ks[7], (DEPTH, N_SSM_GROUPS, SSM_STATE, SSM_GROUP), (2 * SSM_GROUP) ** -0.5),
        "ssm_c_re": nrm(ks[8], (DEPTH, N_SSM_GROUPS, SSM_GROUP, SSM_STATE), (2 * SSM_STATE) ** -0.5),
        "ssm_c_im": nrm(ks[9], (DEPTH, N_SSM_GROUPS, SSM_GROUP, SSM_STATE), (2 * SSM_STATE) ** -0.5),
        "ssm_d": nrm(ks[10], (DEPTH, D_SSM), 1.0),
        "ssm_w_glu": nrm(ks[11], (DEPTH, D_SSM, D_SSM), D_SSM ** -0.5),
        "conv_w": nrm(ks[12], (DEPTH, CONV_K, D_CONV), CONV_K ** -0.5),
        "conv_ln_g": gain(ks[13], D_CONV),
        "conv_ln_b": nrm(ks[14], (DEPTH, D_CONV), 0.01),
        "w_out": nrm(ks[15], (DEPTH, D_MIX, D_MODEL), D_MIX ** -0.5),
        "post_mix_g": gain(ks[16], D_MODEL),
        "pre_ffn_g": gain(ks[17], D_MODEL),
        "ffn_w_up": nrm(ks[18], (DEPTH, D_MODEL, 2 * D_FF), D_MODEL ** -0.5),
        "ffn_conv_w": nrm(ks[19], (DEPTH, FFN_CONV_K, 2 * D_FF), FFN_CONV_K ** -0.5),
        "ffn_w_down": nrm(ks[20], (DEPTH, D_FF, D_MODEL), D_FF ** -0.5),
        "post_ffn_g": gain(ks[21], D_MODEL),
    }


def reference(x, pre_mix_g, w_in, ssm_log_dt, ssm_lam_re, ssm_lam_im, ssm_b_re, ssm_b_im,
              ssm_c_re, ssm_c_im, ssm_d, ssm_w_glu, conv_w, conv_ln_g, conv_ln_b, w_out,
              post_mix_g, pre_ffn_g, ffn_w_up, ffn_conv_w, ffn_w_down, post_ffn_g):
    for i in range(DEPTH):
        h = rms_norm(x, pre_mix_g[i])
        proj = h @ w_in[i]
        u = proj[..., :D_SSM]
        cv = proj[..., D_SSM:D_SSM + D_CONV]
        cg = proj[..., D_SSM + D_CONV:]
        y_ssm = s5_mixer(u, ssm_log_dt[i], ssm_lam_re[i], ssm_lam_im[i], ssm_b_re[i], ssm_b_im[i],
                         ssm_c_re[i], ssm_c_im[i], ssm_d[i], ssm_w_glu[i])
        y_conv = conformer_conv_mixer(cv, cg, conv_w[i], conv_ln_g[i], conv_ln_b[i])
        y = jnp.concatenate([y_ssm, y_conv], axis=-1) @ w_out[i]
        x = x + rms_norm(y, post_mix_g[i])
        h = rms_norm(x, pre_ffn_g[i])
        f = conv_ffn(h, ffn_w_up[i], ffn_conv_w[i], ffn_w_down[i])
        x = x + rms_norm(f, post_ffn_g[i])
    return x
```

```python
import functools

import jax
import jax.numpy as jnp
from jax import lax
from jax.experimental import pallas as pl
from jax.experimental.pallas import tpu as pltpu

F32 = jnp.float32
BF16 = jnp.bfloat16

EPS = 1e-6
SSM_GROUP = 16
SSM_STATE = 64
CONV_K = 31
FFN_CONV_K = 3

LANES = 128
SUBLANES = 8
SSM_CHUNK = 16
GROUPS_PER_BLOCK = LANES // SSM_GROUP
STATE_W = 2 * GROUPS_PER_BLOCK * SSM_STATE
CONV_HALO = 32
VMEM_LIMIT = 56 * 1024 * 1024


def _rms(x, gain):
    return x * lax.rsqrt(jnp.mean(x * x, axis=-1, keepdims=True) + EPS) * gain


def _in_proj_kernel(x_ref, g_ref, w_ref, ucv_ref, z_ref, u_scr, *, d_ssm, d_conv, chunks):
    h = _rms(x_ref[...], g_ref[...]).astype(BF16)
    proj = jnp.dot(h, w_ref[...], preferred_element_type=F32)
    cv = proj[:, d_ssm:d_ssm + d_conv]
    cg = proj[:, d_ssm + d_conv:]
    z_ref[...] = cv * jax.nn.sigmoid(cg)
    for b in range(d_ssm // LANES):
        u_scr[b] = proj[:, b * LANES:(b + 1) * LANES]
        for t in range(SSM_CHUNK):
            rows = u_scr[b, pl.ds(t, chunks, stride=SSM_CHUNK), :]
            ucv_ref[b, :, t * LANES:(t + 1) * LANES] = rows.astype(BF16)


def _in_proj(x2, gain, w_in, *, d_ssm, d_conv, tm):
    n_tok, d_model = x2.shape
    chunks = tm // SSM_CHUNK
    nb = d_ssm // LANES
    return pl.pallas_call(
        functools.partial(_in_proj_kernel, d_ssm=d_ssm, d_conv=d_conv, chunks=chunks),
        grid=(n_tok // tm,),
        in_specs=[
            pl.BlockSpec((tm, d_model), lambda i: (i, 0)),
            pl.BlockSpec((1, d_model), lambda i: (0, 0)),
            pl.BlockSpec(w_in.shape, lambda i: (0, 0)),
        ],
        out_specs=[
            pl.BlockSpec((nb, chunks, SSM_CHUNK * LANES), lambda i: (0, i, 0)),
            pl.BlockSpec((tm, d_conv), lambda i: (i, 0)),
        ],
        out_shape=[
            jax.ShapeDtypeStruct((nb, n_tok // SSM_CHUNK, SSM_CHUNK * LANES), BF16),
            jax.ShapeDtypeStruct((n_tok, d_conv), F32),
        ],
        scratch_shapes=[pltpu.VMEM((nb, tm, LANES), F32)],
        compiler_params=pltpu.CompilerParams(
            dimension_semantics=("arbitrary",), vmem_limit_bytes=VMEM_LIMIT),
        name="in_proj",
    )(x2, gain, w_in)


def _ssm_params(log_dt, lam_re, lam_im, b_re, b_im, c_re, c_im, d):
    n_groups = lam_re.shape[0]
    nb = n_groups // GROUPS_PER_BLOCK
    lam = lax.complex(lam_re.astype(F32), lam_im.astype(F32))
    lam_dt = lam * jnp.exp(log_dt.astype(F32))[:, None]
    steps = jnp.arange(SSM_CHUNK + 1, dtype=F32)[:, None, None]
    pw = jnp.exp(lam_dt[None] * steps)
    b_bar = ((pw[1] - 1.0) / lam)[..., None] * lax.complex(b_re.astype(F32), b_im.astype(F32))
    c = lax.complex(c_re.astype(F32), c_im.astype(F32))

    def blocked(a):
        a = jnp.stack([a.real, a.imag], axis=1)
        a = a.reshape(SSM_CHUNK, 2, nb, GROUPS_PER_BLOCK, SSM_STATE, SSM_GROUP)
        a = a.transpose(2, 0, 5, 1, 3, 4)
        return a.reshape(nb, SSM_CHUNK, SSM_GROUP, STATE_W)

    vb = blocked(pw[SSM_CHUNK - 1::-1][:, :, :, None] * b_bar[None])
    cl = jnp.conj(c.transpose(0, 2, 1)[None] * pw[1:, :, :, None])
    vc = blocked(cl)

    kk = jnp.einsum("ghp,mgp,gpk->mgkh", c, pw[:SSM_CHUNK], b_bar).real
    kk = kk.at[0].add(d.astype(F32).reshape(n_groups, SSM_GROUP)[:, :, None]
                      * jnp.eye(SSM_GROUP, dtype=F32)[None])
    kt = kk.reshape(SSM_CHUNK, nb, GROUPS_PER_BLOCK, SSM_GROUP, SSM_GROUP)
    kt = kt.transpose(1, 0, 3, 2, 4).reshape(nb, SSM_CHUNK, SSM_GROUP, LANES)

    row = jnp.arange(SUBLANES)
    expo = jnp.stack([jnp.where(row >= 1, 1, 0), jnp.where(row >= 2, 2, 0),
                      jnp.where(row >= 4, 4, 0), row + 1])
    keep = jnp.stack([row >= 1, row >= 2, row >= 4, row >= 0])
    ap = jnp.exp(lam_dt[None, None] * (SSM_CHUNK * expo.astype(F32))[:, :, None, None])
    ap = jnp.where(keep[:, :, None, None], ap, 0.0)
    ap = jnp.stack([ap.real, ap.imag], axis=2)
    ap = ap.reshape(4, SUBLANES, 2, nb, GROUPS_PER_BLOCK * SSM_STATE)
    atab = ap.transpose(3, 0, 1, 2, 4).reshape(nb, 4, SUBLANES, STATE_W)
    return kt, vb, vc, atab


def _ssm_kernel(u_ref, kt_ref, vb_ref, vc_ref, at_ref, y_ref, w1_scr, qct_scr, b_scr, s_scr):
    cw = SSM_CHUNK * LANES
    half = STATE_W // 2
    n_chunks = u_ref.shape[0]

    @pl.when(pl.program_id(1) == 0)
    def _build_operators():
        row_g = lax.broadcasted_iota(jnp.int32, (LANES, STATE_W), 0) // SSM_GROUP
        col_g = (lax.broadcasted_iota(jnp.int32, (LANES, STATE_W), 1) % half) // SSM_STATE
        smask = row_g == col_g
        row_k = lax.broadcasted_iota(jnp.int32, (LANES, LANES), 0) // SSM_GROUP
        col_k = lax.broadcasted_iota(jnp.int32, (LANES, LANES), 1) // SSM_GROUP
        kmask = row_k == col_k
        zero_blk = jnp.zeros((LANES, LANES), BF16)
        blocks = []
        for m in range(SSM_CHUNK):
            rows = slice(m * LANES, (m + 1) * LANES)
            vb = jnp.concatenate([vb_ref[m]] * GROUPS_PER_BLOCK, axis=0)
            w1_scr[rows, cw:] = jnp.where(smask, vb, 0.0).astype(BF16)
            vc = jnp.concatenate([vc_ref[m]] * GROUPS_PER_BLOCK, axis=0)
            qct_scr[rows, :] = jnp.where(smask, vc, 0.0).astype(BF16)
            kt = jnp.concatenate([kt_ref[m]] * GROUPS_PER_BLOCK, axis=0)
            blocks.append(jnp.where(kmask, kt, 0.0).astype(BF16))
        for j in range(SSM_CHUNK):
            for k in range(SSM_CHUNK):
                w1_scr[j * LANES:(j + 1) * LANES, k * LANES:(k + 1) * LANES] = (
                    blocks[k - j] if k >= j else zero_blk)

    r = jnp.dot(u_ref[...], w1_scr[...], preferred_element_type=F32)
    bc = pltpu.roll(r[:, cw:], 1, axis=0)
    first_row = lax.broadcasted_iota(jnp.int32, bc.shape, 0) == 0
    b_scr[...] = jnp.where(first_row, 0.0, bc)

    a1, a2, a4, ap = at_ref[0], at_ref[1], at_ref[2], at_ref[3]

    def cmul_add(xr, xi, ar, ai, sr, si):
        return xr + ar * sr - ai * si, xi + ar * si + ai * sr

    def block_scan(i, carry):
        cr, ci = carry
        r0 = pl.multiple_of(i * SUBLANES, SUBLANES)
        x = b_scr[pl.ds(r0, SUBLANES), :]
        xr, xi = x[:, :half], x[:, half:]
        for d, tab in ((1, a1), (2, a2), (4, a4)):
            sr = pltpu.roll(xr, d, axis=0)
            si = pltpu.roll(xi, d, axis=0)
            xr, xi = cmul_add(xr, xi, tab[:, :half], tab[:, half:], sr, si)
        xr, xi = cmul_add(xr, xi, ap[:, :half], ap[:, half:], cr, ci)
        s_scr[pl.ds(r0, SUBLANES), :] = jnp.concatenate([xr, xi], axis=1)
        return xr[SUBLANES - 1:, :], xi[SUBLANES - 1:, :]

    zero = jnp.zeros((1, half), F32)
    lax.fori_loop(0, n_chunks // SUBLANES, block_scan, (zero, zero))

    y = r[:, :cw] + lax.dot_general(s_scr[...].astype(BF16), qct_scr[...], (((1,), (1,)), ((), ())),
                                    preferred_element_type=F32)
    y_ref[...] = y.astype(BF16)


def _ssm(ucv, kt, vb, vc, atab, *, batch):
    nb, n_chunks, cw = ucv.shape
    cps = n_chunks // batch
    return pl.pallas_call(
        _ssm_kernel,
        grid=(nb, batch),
        in_specs=[
            pl.BlockSpec((None, cps, cw), lambda b, s: (b, s, 0)),
            pl.BlockSpec((None,) + kt.shape[1:], lambda b, s: (b, 0, 0, 0)),
            pl.BlockSpec((None,) + vb.shape[1:], lambda b, s: (b, 0, 0, 0)),
            pl.BlockSpec((None,) + vc.shape[1:], lambda b, s: (b, 0, 0, 0)),
            pl.BlockSpec((None,) + atab.shape[1:], lambda b, s: (b, 0, 0, 0)),
        ],
        out_specs=pl.BlockSpec((None, cps, cw), lambda b, s: (b, s, 0)),
        out_shape=jax.ShapeDtypeStruct(ucv.shape, BF16),
        scratch_shapes=[
            pltpu.VMEM((cw, cw + STATE_W), BF16),
            pltpu.VMEM((cw, STATE_W), BF16),
            pltpu.VMEM((cps, STATE_W), F32),
            pltpu.VMEM((cps, STATE_W), F32),
        ],
        compiler_params=pltpu.CompilerParams(
            dimension_semantics=("arbitrary", "arbitrary"), vmem_limit_bytes=VMEM_LIMIT),
        name="ssm",
    )(ucv, kt, vb, vc, atab)


def _mix_out_kernel(ycv_ref, z_ref, zh_ref, x_ref, cw_ref, lng_ref, lnb_ref, wglu_ref,
                    wout_ref, pg_ref, o_ref, y_scr, z_scr, *, tiles_per_seq):
    nb, chunks, _ = ycv_ref.shape
    tm, d_conv = z_ref.shape
    d_ssm = nb * LANES

    for b in range(nb):
        for t in range(SSM_CHUNK):
            rows = ycv_ref[b, :, t * LANES:(t + 1) * LANES]
            y_scr[b, pl.ds(t, chunks, stride=SSM_CHUNK), :] = rows.astype(F32)
    y = jax.nn.gelu(jnp.concatenate([y_scr[b] for b in range(nb)], axis=1))
    gate = jnp.dot(y.astype(BF16), wglu_ref[...], preferred_element_type=F32)
    a = y * jax.nn.sigmoid(gate)

    seq_start = (pl.program_id(0) % tiles_per_seq) == 0
    z_scr[0:CONV_HALO, :] = jnp.where(seq_start, 0.0, zh_ref[...])
    z_scr[CONV_HALO:, :] = z_ref[...]
    acc = jnp.zeros((tm, d_conv), F32)
    for k in range(CONV_K):
        acc = acc + cw_ref[k:k + 1, :] * z_scr[pl.ds(CONV_HALO - (CONV_K - 1) + k, tm), :]
    mu = jnp.mean(acc, axis=-1, keepdims=True)
    xc = acc - mu
    zn = xc * lax.rsqrt(jnp.mean(xc * xc, axis=-1, keepdims=True) + EPS)
    c = jax.nn.silu(zn * lng_ref[...] + lnb_ref[...])

    yo = (jnp.dot(a.astype(BF16), wout_ref[0:d_ssm, :], preferred_element_type=F32)
          + jnp.dot(c.astype(BF16), wout_ref[d_ssm:, :], preferred_element_type=F32))
    o_ref[...] = x_ref[...] + _rms(yo, pg_ref[...])


def _mix_out(ycv, z, x2, conv_w, ln_g, ln_b, w_glu, w_out, post_g, *, tm, seq):
    n_tok, d_model = x2.shape
    nb = ycv.shape[0]
    d_conv = z.shape[1]
    chunks = tm // SSM_CHUNK
    halo_blocks = tm // CONV_HALO
    const = lambda i: (0, 0)
    return pl.pallas_call(
        functools.partial(_mix_out_kernel, tiles_per_seq=seq // tm),
        grid=(n_tok // tm,),
        in_specs=[
            pl.BlockSpec((nb, chunks, SSM_CHUNK * LANES), lambda i: (0, i, 0)),
            pl.BlockSpec((tm, d_conv), lambda i: (i, 0)),
            pl.BlockSpec((CONV_HALO, d_conv), lambda i: (jnp.maximum(i * halo_blocks - 1, 0), 0)),
            pl.BlockSpec((tm, d_model), lambda i: (i, 0)),
            pl.BlockSpec(conv_w.shape, const),
            pl.BlockSpec(ln_g.shape, const),
            pl.BlockSpec(ln_b.shape, const),
            pl.BlockSpec(w_glu.shape, const),
            pl.BlockSpec(w_out.shape, const),
            pl.BlockSpec(post_g.shape, const),
        ],
        out_specs=pl.BlockSpec((tm, d_model), lambda i: (i, 0)),
        out_shape=jax.ShapeDtypeStruct((n_tok, d_model), F32),
        scratch_shapes=[
            pltpu.VMEM((nb, tm, LANES), F32),
            pltpu.VMEM((tm + CONV_HALO, d_conv), F32),
        ],
        compiler_params=pltpu.CompilerParams(
            dimension_semantics=("arbitrary",), vmem_limit_bytes=VMEM_LIMIT),
        name="mix_out",
    )(ycv, z, z, x2, conv_w, ln_g, ln_b, w_glu, w_out, post_g)


def _ffn_kernel(x_ref, g_ref, wg_ref, wv_ref, cg_ref, cv_ref, wd_ref, pg_ref, o_ref,
                h_scr, acc_scr, ext_scr, carry_scr, *, tiles_per_seq):
    i = pl.program_id(0)
    j = pl.program_id(1)
    tm = x_ref.shape[0]

    @pl.when(j == 0)
    def _start_tile():
        h_scr[...] = _rms(x_ref[...], g_ref[...]).astype(BF16)
        acc_scr[...] = jnp.zeros_like(acc_scr)

    seq_start = (i % tiles_per_seq) == 0
    h = h_scr[...]

    def conv(w_ref, cw_ref, slot):
        up = jnp.dot(h, w_ref[...], preferred_element_type=F32)
        ext_scr[slot, 0:SUBLANES, :] = jnp.where(seq_start, 0.0, carry_scr[j, slot])
        ext_scr[slot, SUBLANES:, :] = up
        carry_scr[j, slot] = up[tm - SUBLANES:, :]
        out = cw_ref[FFN_CONV_K - 1:FFN_CONV_K, :] * up
        for k in range(FFN_CONV_K - 1):
            shift = FFN_CONV_K - 1 - k
            out = out + cw_ref[k:k + 1, :] * ext_scr[slot, pl.ds(SUBLANES - shift, tm), :]
        return out

    hidden = jax.nn.gelu(conv(wg_ref, cg_ref, 0)) * conv(wv_ref, cv_ref, 1)
    acc_scr[...] += jnp.dot(hidden.astype(BF16), wd_ref[...], preferred_element_type=F32)

    @pl.when(j == pl.num_programs(1) - 1)
    def _finish_tile():
        o_ref[...] = x_ref[...] + _rms(acc_scr[...], pg_ref[...])


def _ffn(x1, pre_g, w_up, conv_w, w_down, post_g, *, tm, tf, seq):
    n_tok, d_model = x1.shape
    d_ff = w_down.shape[0]
    n_ff = d_ff // tf
    const = lambda i, j: (0, 0)
    return pl.pallas_call(
        functools.partial(_ffn_kernel, tiles_per_seq=seq // tm),
        grid=(n_tok // tm, n_ff),
        in_specs=[
            pl.BlockSpec((tm, d_model), lambda i, j: (i, 0)),
            pl.BlockSpec(pre_g.shape, const),
            pl.BlockSpec((d_model, tf), lambda i, j: (0, j)),
            pl.BlockSpec((d_model, tf), lambda i, j: (0, j + n_ff)),
            pl.BlockSpec((FFN_CONV_K, tf), lambda i, j: (0, j)),
            pl.BlockSpec((FFN_CONV_K, tf), lambda i, j: (0, j + n_ff)),
            pl.BlockSpec((tf, d_model), lambda i, j: (j, 0)),
            pl.BlockSpec(post_g.shape, const),
        ],
        out_specs=pl.BlockSpec((tm, d_model), lambda i, j: (i, 0)),
        out_shape=jax.ShapeDtypeStruct((n_tok, d_model), F32),
        scratch_shapes=[
            pltpu.VMEM((tm, d_model), BF16),
            pltpu.VMEM((tm, d_model), F32),
            pltpu.VMEM((2, tm + SUBLANES, tf), F32),
            pltpu.VMEM((n_ff, 2, SUBLANES, tf), F32),
        ],
        compiler_params=pltpu.CompilerParams(
            dimension_semantics=("arbitrary", "arbitrary"), vmem_limit_bytes=VMEM_LIMIT),
        name="ffn",
    )(x1, pre_g, w_up, w_up, conv_w, conv_w, w_down, post_g)


def _layer(x, pre_mix_g, w_in, log_dt, lam_re, lam_im, b_re, b_im, c_re, c_im, d, w_glu,
           conv_w, ln_g, ln_b, w_out, post_mix_g, pre_ffn_g, w_up, ffn_conv_w, w_down,
           post_ffn_g, *, tm, tf):
    batch, seq, d_model = x.shape
    d_ssm = w_glu.shape[0]
    d_conv = conv_w.shape[1]
    assert seq % tm == 0 and tm % (SSM_CHUNK * SUBLANES) == 0 and tm % CONV_HALO == 0
    assert d_ssm % LANES == 0 and (seq // SSM_CHUNK) % SUBLANES == 0
    row = lambda v: v.reshape(1, -1).astype(F32)

    x2 = x.reshape(batch * seq, d_model)
    ucv, z = _in_proj(x2, row(pre_mix_g), w_in.astype(BF16), d_ssm=d_ssm, d_conv=d_conv, tm=tm)
    kt, vb, vc, atab = _ssm_params(log_dt, lam_re, lam_im, b_re, b_im, c_re, c_im, d)
    ycv = _ssm(ucv, kt, vb, vc, atab, batch=batch)
    x1 = _mix_out(ycv, z, x2, conv_w.astype(F32), row(ln_g), row(ln_b), w_glu.astype(BF16),
                  w_out.astype(BF16), row(post_mix_g), tm=tm, seq=seq)
    out = _ffn(x1, row(pre_ffn_g), w_up.astype(BF16), ffn_conv_w.astype(F32),
               w_down.astype(BF16), row(post_ffn_g), tm=tm, tf=tf, seq=seq)
    return out.reshape(batch, seq, d_model)


def kernel(x, pre_mix_g, w_in, ssm_log_dt, ssm_lam_re, ssm_lam_im, ssm_b_re, ssm_b_im,
           ssm_c_re, ssm_c_im, ssm_d, ssm_w_glu, conv_w, conv_ln_g, conv_ln_b, w_out,
           post_mix_g, pre_ffn_g, ffn_w_up, ffn_conv_w, ffn_w_down, post_ffn_g):
    for i in range(pre_mix_g.shape[0]):
        x = _layer(x, pre_mix_g[i], w_in[i], ssm_log_dt[i], ssm_lam_re[i], ssm_lam_im[i],
                   ssm_b_re[i], ssm_b_im[i], ssm_c_re[i], ssm_c_im[i], ssm_d[i], ssm_w_glu[i],
                   conv_w[i], conv_ln_g[i], conv_ln_b[i], w_out[i], post_mix_g[i],
                   pre_ffn_g[i], ffn_w_up[i], ffn_conv_w[i], ffn_w_down[i], post_ffn_g[i],
                   tm=512, tf=512)
    return x
```

```python
import functools

import jax
import jax.numpy as jnp
from jax import lax
from jax.experimental import pallas as pl
from jax.experimental.pallas import tpu as pltpu

F32 = jnp.float32
BF16 = jnp.bfloat16

EPS = 1e-6
SSM_GROUP = 16
SSM_STATE = 64
CONV_K = 31
FFN_CONV_K = 3

LANES = 128
SUBLANES = 8
SSM_CHUNK = 16
GROUPS_PER_BLOCK = LANES // SSM_GROUP
STATE_W = 2 * GROUPS_PER_BLOCK * SSM_STATE
CONV_HALO = 32
CONV_COLS = 256
VMEM_LIMIT = 56 * 1024 * 1024


def _rms(x, gain):
    return x * lax.rsqrt(jnp.mean(x * x, axis=-1, keepdims=True) + EPS) * gain


def _in_proj_kernel(x_ref, g_ref, w_ref, cw_ref, lng_ref, lnb_ref, ucv_ref, c_ref,
                    u_scr, z_scr, zs_scr, co_scr, *, d_ssm, d_conv, chunks, tiles_per_seq):
    tm = x_ref.shape[0]
    i = pl.program_id(0)

    @pl.when(i == 0)
    def _init_context():
        z_scr[...] = jnp.zeros_like(z_scr)

    seq_start = (i % tiles_per_seq) == 0
    z_scr[0:CONV_HALO, :] = jnp.where(seq_start, 0.0, z_scr[tm:tm + CONV_HALO, :])

    h = _rms(x_ref[...], g_ref[...]).astype(BF16)
    shifted_rows = tm + CONV_HALO - SUBLANES
    for cb in range(d_conv // CONV_COLS):
        cols = slice(cb * CONV_COLS, (cb + 1) * CONV_COLS)
        v0 = d_ssm + cb * CONV_COLS
        g0 = d_ssm + d_conv + cb * CONV_COLS
        cv = jnp.dot(h, w_ref[:, v0:v0 + CONV_COLS], preferred_element_type=F32)
        cg = jnp.dot(h, w_ref[:, g0:g0 + CONV_COLS], preferred_element_type=F32)
        z_scr[CONV_HALO:, cols] = cv * jax.nn.sigmoid(cg)
        for r in range(1, SUBLANES):
            zs_scr[r - 1] = z_scr[pl.ds(r, shifted_rows), cols]
        acc = None
        for k in range(CONV_K):
            q, r = divmod(CONV_HALO - (CONV_K - 1) + k, SUBLANES)
            if r == 0:
                src = z_scr[pl.ds(q * SUBLANES, tm), cols]
            else:
                src = zs_scr[r - 1, pl.ds(q * SUBLANES, tm), :]
            term = cw_ref[k:k + 1, cols] * src
            acc = term if acc is None else acc + term
        co_scr[:, cols] = acc

    u = jnp.dot(h, w_ref[:, 0:d_ssm], preferred_element_type=F32)
    for b in range(d_ssm // LANES):
        u_scr[b] = u[:, b * LANES:(b + 1) * LANES]
        for t in range(SSM_CHUNK):
            rows = u_scr[b, pl.ds(t, chunks, stride=SSM_CHUNK), :]
            ucv_ref[b, :, t * LANES:(t + 1) * LANES] = rows.astype(BF16)

    co = co_scr[...]
    mu = jnp.mean(co, axis=-1, keepdims=True)
    xc = co - mu
    zn = xc * lax.rsqrt(jnp.mean(xc * xc, axis=-1, keepdims=True) + EPS)
    c_ref[...] = jax.nn.silu(zn * lng_ref[...] + lnb_ref[...]).astype(BF16)


def _in_proj(x2, gain, w_in, conv_w, ln_g, ln_b, *, d_ssm, d_conv, tm, seq):
    n_tok, d_model = x2.shape
    chunks = tm // SSM_CHUNK
    nb = d_ssm // LANES
    const = lambda i: (0, 0)
    return pl.pallas_call(
        functools.partial(_in_proj_kernel, d_ssm=d_ssm, d_conv=d_conv, chunks=chunks,
                          tiles_per_seq=seq // tm),
        grid=(n_tok // tm,),
        in_specs=[
            pl.BlockSpec((tm, d_model), lambda i: (i, 0)),
            pl.BlockSpec((1, d_model), const),
            pl.BlockSpec(w_in.shape, const),
            pl.BlockSpec(conv_w.shape, const),
            pl.BlockSpec(ln_g.shape, const),
            pl.BlockSpec(ln_b.shape, const),
        ],
        out_specs=[
            pl.BlockSpec((nb, chunks, SSM_CHUNK * LANES), lambda i: (0, i, 0)),
            pl.BlockSpec((tm, d_conv), lambda i: (i, 0)),
        ],
        out_shape=[
            jax.ShapeDtypeStruct((nb, n_tok // SSM_CHUNK, SSM_CHUNK * LANES), BF16),
            jax.ShapeDtypeStruct((n_tok, d_conv), BF16),
        ],
        scratch_shapes=[
            pltpu.VMEM((nb, tm, LANES), F32),
            pltpu.VMEM((tm + CONV_HALO, d_conv), F32),
            pltpu.VMEM((SUBLANES - 1, tm + CONV_HALO - SUBLANES, CONV_COLS), F32),
            pltpu.VMEM((tm, d_conv), F32),
        ],
        compiler_params=pltpu.CompilerParams(
            dimension_semantics=("arbitrary",), vmem_limit_bytes=VMEM_LIMIT),
        name="in_proj",
    )(x2, gain, w_in, conv_w, ln_g, ln_b)


def _ssm_params(log_dt, lam_re, lam_im, b_re, b_im, c_re, c_im, d):
    n_groups = lam_re.shape[0]
    nb = n_groups // GROUPS_PER_BLOCK
    lam = lax.complex(lam_re.astype(F32), lam_im.astype(F32))
    lam_dt = lam * jnp.exp(log_dt.astype(F32))[:, None]
    steps = jnp.arange(SSM_CHUNK + 1, dtype=F32)[:, None, None]
    pw = jnp.exp(lam_dt[None] * steps)
    b_bar = ((pw[1] - 1.0) / lam)[..., None] * lax.complex(b_re.astype(F32), b_im.astype(F32))
    c = lax.complex(c_re.astype(F32), c_im.astype(F32))

    def blocked(a):
        a = jnp.stack([a.real, a.imag], axis=1)
        a = a.reshape(SSM_CHUNK, 2, nb, GROUPS_PER_BLOCK, SSM_STATE, SSM_GROUP)
        a = a.transpose(2, 0, 5, 1, 3, 4)
        return a.reshape(nb, SSM_CHUNK, SSM_GROUP, STATE_W)

    vb = blocked(pw[SSM_CHUNK - 1::-1][:, :, :, None] * b_bar[None])
    cl = jnp.conj(c.transpose(0, 2, 1)[None] * pw[1:, :, :, None])
    vc = blocked(cl)

    kk = jnp.einsum("ghp,mgp,gpk->mgkh", c, pw[:SSM_CHUNK], b_bar).real
    kk = kk.at[0].add(d.astype(F32).reshape(n_groups, SSM_GROUP)[:, :, None]
                      * jnp.eye(SSM_GROUP, dtype=F32)[None])
    kt = kk.reshape(SSM_CHUNK, nb, GROUPS_PER_BLOCK, SSM_GROUP, SSM_GROUP)
    kt = kt.transpose(1, 0, 3, 2, 4).reshape(nb, SSM_CHUNK, SSM_GROUP, LANES)

    row = jnp.arange(SUBLANES)
    expo = jnp.stack([jnp.where(row >= 1, 1, 0), jnp.where(row >= 2, 2, 0),
                      jnp.where(row >= 4, 4, 0), row + 1])
    keep = jnp.stack([row >= 1, row >= 2, row >= 4, row >= 0])
    ap = jnp.exp(lam_dt[None, None] * (SSM_CHUNK * expo.astype(F32))[:, :, None, None])
    ap = jnp.where(keep[:, :, None, None], ap, 0.0)
    ap = jnp.stack([ap.real, ap.imag], axis=2)
    ap = ap.reshape(4, SUBLANES, 2, nb, GROUPS_PER_BLOCK * SSM_STATE)
    atab = ap.transpose(3, 0, 1, 2, 4).reshape(nb, 4, SUBLANES, STATE_W)
    return kt, vb, vc, atab


def _ssm_kernel(u_ref, kt_ref, vb_ref, vc_ref, at_ref, y_ref, w1_scr, qct_scr, b_scr, s_scr):
    cw = SSM_CHUNK * LANES
    half = STATE_W // 2
    n_chunks = u_ref.shape[0]

    @pl.when(pl.program_id(1) == 0)
    def _build_operators():
        row_g = lax.broadcasted_iota(jnp.int32, (LANES, STATE_W), 0) // SSM_GROUP
        col_g = (lax.broadcasted_iota(jnp.int32, (LANES, STATE_W), 1) % half) // SSM_STATE
        smask = row_g == col_g
        row_k = lax.broadcasted_iota(jnp.int32, (LANES, LANES), 0) // SSM_GROUP
        col_k = lax.broadcasted_iota(jnp.int32, (LANES, LANES), 1) // SSM_GROUP
        kmask = row_k == col_k
        zero_blk = jnp.zeros((LANES, LANES), BF16)
        blocks = []
        for m in range(SSM_CHUNK):
            rows = slice(m * LANES, (m + 1) * LANES)
            vb = jnp.concatenate([vb_ref[m]] * GROUPS_PER_BLOCK, axis=0)
            w1_scr[rows, cw:] = jnp.where(smask, vb, 0.0).astype(BF16)
            vc = jnp.concatenate([vc_ref[m]] * GROUPS_PER_BLOCK, axis=0)
            qct_scr[rows, :] = jnp.where(smask, vc, 0.0).astype(BF16)
            kt = jnp.concatenate([kt_ref[m]] * GROUPS_PER_BLOCK, axis=0)
            blocks.append(jnp.where(kmask, kt, 0.0).astype(BF16))
        for j in range(SSM_CHUNK):
            for k in range(SSM_CHUNK):
                w1_scr[j * LANES:(j + 1) * LANES, k * LANES:(k + 1) * LANES] = (
                    blocks[k - j] if k >= j else zero_blk)

    r = jnp.dot(u_ref[...], w1_scr[...], preferred_element_type=F32)
    bc = pltpu.roll(r[:, cw:], 1, axis=0)
    first_row = lax.broadcasted_iota(jnp.int32, bc.shape, 0) == 0
    b_scr[...] = jnp.where(first_row, 0.0, bc)

    a1, a2, a4, ap = at_ref[0], at_ref[1], at_ref[2], at_ref[3]

    def cmul_add(xr, xi, ar, ai, sr, si):
        return xr + ar * sr - ai * si, xi + ar * si + ai * sr

    def block_scan(i, carry):
        cr, ci = carry
        r0 = pl.multiple_of(i * SUBLANES, SUBLANES)
        x = b_scr[pl.ds(r0, SUBLANES), :]
        xr, xi = x[:, :half], x[:, half:]
        for d, tab in ((1, a1), (2, a2), (4, a4)):
            sr = pltpu.roll(xr, d, axis=0)
            si = pltpu.roll(xi, d, axis=0)
            xr, xi = cmul_add(xr, xi, tab[:, :half], tab[:, half:], sr, si)
        xr, xi = cmul_add(xr, xi, ap[:, :half], ap[:, half:], cr, ci)
        s_scr[pl.ds(r0, SUBLANES), :] = jnp.concatenate([xr, xi], axis=1)
        return xr[SUBLANES - 1:, :], xi[SUBLANES - 1:, :]

    zero = jnp.zeros((1, half), F32)
    lax.fori_loop(0, n_chunks // SUBLANES, block_scan, (zero, zero))

    y = r[:, :cw] + lax.dot_general(s_scr[...].astype(BF16), qct_scr[...], (((1,), (1,)), ((), ())),
                                    preferred_element_type=F32)
    y_ref[...] = y.astype(BF16)


def _ssm(ucv, kt, vb, vc, atab, *, batch):
    nb, n_chunks, cw = ucv.shape
    cps = n_chunks // batch
    return pl.pallas_call(
        _ssm_kernel,
        grid=(nb, batch),
        in_specs=[
            pl.BlockSpec((None, cps, cw), lambda b, s: (b, s, 0)),
            pl.BlockSpec((None,) + kt.shape[1:], lambda b, s: (b, 0, 0, 0)),
            pl.BlockSpec((None,) + vb.shape[1:], lambda b, s: (b, 0, 0, 0)),
            pl.BlockSpec((None,) + vc.shape[1:], lambda b, s: (b, 0, 0, 0)),
            pl.BlockSpec((None,) + atab.shape[1:], lambda b, s: (b, 0, 0, 0)),
        ],
        out_specs=pl.BlockSpec((None, cps, cw), lambda b, s: (b, s, 0)),
        out_shape=jax.ShapeDtypeStruct(ucv.shape, BF16),
        scratch_shapes=[
            pltpu.VMEM((cw, cw + STATE_W), BF16),
            pltpu.VMEM((cw, STATE_W), BF16),
            pltpu.VMEM((cps, STATE_W), F32),
            pltpu.VMEM((cps, STATE_W), F32),
        ],
        compiler_params=pltpu.CompilerParams(
            dimension_semantics=("arbitrary", "arbitrary"), vmem_limit_bytes=VMEM_LIMIT),
        name="ssm",
    )(ucv, kt, vb, vc, atab)


def _mix_out_kernel(ycv_ref, c_ref, x_ref, wglu_ref, wout_ref, pg_ref, fg_ref, o_ref, h_ref,
                    y_scr):
    nb, chunks, _ = ycv_ref.shape
    d_ssm = nb * LANES

    for b in range(nb):
        for t in range(SSM_CHUNK):
            rows = ycv_ref[b, :, t * LANES:(t + 1) * LANES]
            y_scr[b, pl.ds(t, chunks, stride=SSM_CHUNK), :] = rows.astype(F32)
    y = jax.nn.gelu(jnp.concatenate([y_scr[b] for b in range(nb)], axis=1))
    gate = jnp.dot(y.astype(BF16), wglu_ref[...], preferred_element_type=F32)
    a = y * jax.nn.sigmoid(gate)

    yo = (jnp.dot(a.astype(BF16), wout_ref[0:d_ssm, :], preferred_element_type=F32)
          + jnp.dot(c_ref[...], wout_ref[d_ssm:, :], preferred_element_type=F32))
    x1 = x_ref[...] + _rms(yo, pg_ref[...])
    o_ref[...] = x1
    h_ref[...] = _rms(x1, fg_ref[...]).astype(BF16)


def _mix_out(ycv, c, x2, w_glu, w_out, post_g, pre_ffn_g, *, tm):
    n_tok, d_model = x2.shape
    nb = ycv.shape[0]
    d_conv = c.shape[1]
    chunks = tm // SSM_CHUNK
    const = lambda i: (0, 0)
    return pl.pallas_call(
        _mix_out_kernel,
        grid=(n_tok // tm,),
        in_specs=[
            pl.BlockSpec((nb, chunks, SSM_CHUNK * LANES), lambda i: (0, i, 0)),
            pl.BlockSpec((tm, d_conv), lambda i: (i, 0)),
            pl.BlockSpec((tm, d_model), lambda i: (i, 0)),
            pl.BlockSpec(w_glu.shape, const),
            pl.BlockSpec(w_out.shape, const),
            pl.BlockSpec(post_g.shape, const),
            pl.BlockSpec(pre_ffn_g.shape, const),
        ],
        out_specs=[
            pl.BlockSpec((tm, d_model), lambda i: (i, 0)),
            pl.BlockSpec((tm, d_model), lambda i: (i, 0)),
        ],
        out_shape=[
            jax.ShapeDtypeStruct((n_tok, d_model), F32),
            jax.ShapeDtypeStruct((n_tok, d_model), BF16),
        ],
        scratch_shapes=[pltpu.VMEM((nb, tm, LANES), F32)],
        compiler_params=pltpu.CompilerParams(
            dimension_semantics=("arbitrary",), vmem_limit_bytes=VMEM_LIMIT),
        name="mix_out",
    )(ycv, c, x2, w_glu, w_out, post_g, pre_ffn_g)


def _ffn_kernel(x_ref, h_ref, wg_ref, wv_ref, cg_ref, cv_ref, wd_ref, pg_ref, o_ref,
                acc_scr, ext_scr, carry_scr, *, tiles_per_seq):
    i = pl.program_id(0)
    j = pl.program_id(1)
    tm = x_ref.shape[0]

    @pl.when(j == 0)
    def _start_tile():
        acc_scr[...] = jnp.zeros_like(acc_scr)

    seq_start = (i % tiles_per_seq) == 0
    h = h_ref[...]

    def conv(w_ref, cw_ref, slot):
        up = jnp.dot(h, w_ref[...], preferred_element_type=F32)
        ext_scr[slot, 0:SUBLANES, :] = jnp.where(seq_start, 0.0, carry_scr[j, slot])
        ext_scr[slot, SUBLANES:, :] = up
        carry_scr[j, slot] = up[tm - SUBLANES:, :]
        out = cw_ref[FFN_CONV_K - 1:FFN_CONV_K, :] * up
        for k in range(FFN_CONV_K - 1):
            shift = FFN_CONV_K - 1 - k
            out = out + cw_ref[k:k + 1, :] * ext_scr[slot, pl.ds(SUBLANES - shift, tm), :]
        return out

    hidden = jax.nn.gelu(conv(wg_ref, cg_ref, 0)) * conv(wv_ref, cv_ref, 1)
    acc_scr[...] += jnp.dot(hidden.astype(BF16), wd_ref[...], preferred_element_type=F32)

    @pl.when(j == pl.num_programs(1) - 1)
    def _finish_tile():
        o_ref[...] = x_ref[...] + _rms(acc_scr[...], pg_ref[...])


def _ffn(x1, h2, w_up, conv_w, w_down, post_g, *, tm, tf, seq):
    n_tok, d_model = x1.shape
    d_ff = w_down.shape[0]
    n_ff = d_ff // tf
    const = lambda i, j: (0, 0)
    return pl.pallas_call(
        functools.partial(_ffn_kernel, tiles_per_seq=seq // tm),
        grid=(n_tok // tm, n_ff),
        in_specs=[
            pl.BlockSpec((tm, d_model), lambda i, j: (i, 0)),
            pl.BlockSpec((tm, d_model), lambda i, j: (i, 0)),
            pl.BlockSpec((d_model, tf), lambda i, j: (0, j)),
            pl.BlockSpec((d_model, tf), lambda i, j: (0, j + n_ff)),
            pl.BlockSpec((FFN_CONV_K, tf), lambda i, j: (0, j)),
            pl.BlockSpec((FFN_CONV_K, tf), lambda i, j: (0, j + n_ff)),
            pl.BlockSpec((tf, d_model), lambda i, j: (j, 0)),
            pl.BlockSpec(post_g.shape, const),
        ],
        out_specs=pl.BlockSpec((tm, d_model), lambda i, j: (i, 0)),
        out_shape=jax.ShapeDtypeStruct((n_tok, d_model), F32),
        scratch_shapes=[
            pltpu.VMEM((tm, d_model), F32),
            pltpu.VMEM((2, tm + SUBLANES, tf), F32),
            pltpu.VMEM((n_ff, 2, SUBLANES, tf), F32),
        ],
        compiler_params=pltpu.CompilerParams(
            dimension_semantics=("arbitrary", "arbitrary"), vmem_limit_bytes=VMEM_LIMIT),
        name="ffn",
    )(x1, h2, w_up, w_up, conv_w, conv_w, w_down, post_g)


def _layer(x, pre_mix_g, w_in, log_dt, lam_re, lam_im, b_re, b_im, c_re, c_im, d, w_glu,
           conv_w, ln_g, ln_b, w_out, post_mix_g, pre_ffn_g, w_up, ffn_conv_w, w_down,
           post_ffn_g, *, tm, tf):
    batch, seq, d_model = x.shape
    d_ssm = w_glu.shape[0]
    d_conv = conv_w.shape[1]
    assert seq % tm == 0 and tm % (SSM_CHUNK * SUBLANES) == 0 and tm % CONV_HALO == 0
    assert d_ssm % LANES == 0 and (seq // SSM_CHUNK) % SUBLANES == 0
    assert d_conv % CONV_COLS == 0
    row = lambda v: v.reshape(1, -1).astype(F32)

    x2 = x.reshape(batch * seq, d_model)
    ucv, c = _in_proj(x2, row(pre_mix_g), w_in.astype(BF16), conv_w.astype(F32), row(ln_g),
                      row(ln_b), d_ssm=d_ssm, d_conv=d_conv, tm=tm, seq=seq)
    kt, vb, vc, atab = _ssm_params(log_dt, lam_re, lam_im, b_re, b_im, c_re, c_im, d)
    ycv = _ssm(ucv, kt, vb, vc, atab, batch=batch)
    x1, h2 = _mix_out(ycv, c, x2, w_glu.astype(BF16), w_out.astype(BF16), row(post_mix_g),
                      row(pre_ffn_g), tm=tm)
    out = _ffn(x1, h2, w_up.astype(BF16), ffn_conv_w.astype(F32), w_down.astype(BF16),
               row(post_ffn_g), tm=tm, tf=tf, seq=seq)
    return out.reshape(batch, seq, d_model)


def kernel(x, pre_mix_g, w_in, ssm_log_dt, ssm_lam_re, ssm_lam_im, ssm_b_re, ssm_b_im,
           ssm_c_re, ssm_c_im, ssm_d, ssm_w_glu, conv_w, conv_ln_g, conv_ln_b, w_out,
           post_mix_g, pre_ffn_g, ffn_w_up, ffn_conv_w, ffn_w_down, post_ffn_g):
    for i in range(pre_mix_g.shape[0]):
        x = _layer(x, pre_mix_g[i], w_in[i], ssm_log_dt[i], ssm_lam_re[i], ssm_lam_im[i],
                   ssm_b_re[i], ssm_b_im[i], ssm_c_re[i], ssm_c_im[i], ssm_d[i], ssm_w_glu[i],
                   conv_w[i], conv_ln_g[i], conv_ln_b[i], w_out[i], post_mix_g[i],
                   pre_ffn_g[i], ffn_w_up[i], ffn_conv_w[i], ffn_w_down[i], post_ffn_g[i],
                   tm=512, tf=512)
    return x
```

```python
import functools

import jax
import jax.numpy as jnp
from jax import lax
from jax.experimental import pallas as pl
from jax.experimental.pallas import tpu as pltpu

F32 = jnp.float32
BF16 = jnp.bfloat16

EPS = 1e-6
SSM_GROUP = 16
SSM_STATE = 64
CONV_K = 31
FFN_CONV_K = 3

LANES = 128
SUBLANES = 8
SSM_CHUNK = 16
GROUPS_PER_BLOCK = LANES // SSM_GROUP
STATE_W = 2 * GROUPS_PER_BLOCK * SSM_STATE
CONV_HALO = 32
MXU_COLS = 256
CONV_COLS = MXU_COLS
FFN_COLS = MXU_COLS
VMEM_LIMIT = 56 * 1024 * 1024


def _rms(x, gain):
    return x * lax.rsqrt(jnp.mean(x * x, axis=-1, keepdims=True) + EPS) * gain


def _in_proj_kernel(x_ref, g_ref, w_ref, cw_ref, lng_ref, lnb_ref, ucv_ref, c_ref,
                    u_scr, zs_scr, co_scr, *z_scrs, d_ssm, d_conv, chunks, tiles_per_seq):
    tm = x_ref.shape[0]
    i = pl.program_id(0)

    @pl.when(i == 0)
    def _init_context():
        for z_scr in z_scrs:
            z_scr[...] = jnp.zeros_like(z_scr)

    seq_start = (i % tiles_per_seq) == 0
    for z_scr in z_scrs:
        z_scr[0:CONV_HALO, :] = jnp.where(seq_start, 0.0, z_scr[tm:tm + CONV_HALO, :])

    h = _rms(x_ref[...], g_ref[...]).astype(BF16)
    shifted_rows = tm + CONV_HALO - SUBLANES
    n_pass = d_conv // CONV_COLS
    u_cols = d_ssm // n_pass

    def glu_pass(cb):
        v0 = d_ssm + cb * CONV_COLS
        g0 = d_ssm + d_conv + cb * CONV_COLS
        cv = jnp.dot(h, w_ref[:, v0:v0 + CONV_COLS], preferred_element_type=F32)
        cg = jnp.dot(h, w_ref[:, g0:g0 + CONV_COLS], preferred_element_type=F32)
        z_scrs[cb][CONV_HALO:, :] = cv * jax.nn.sigmoid(cg)

    def ssm_input_pass(cb):
        u = jnp.dot(h, w_ref[:, cb * u_cols:(cb + 1) * u_cols], preferred_element_type=F32)
        for bl in range(u_cols // LANES):
            b = cb * (u_cols // LANES) + bl
            u_scr[b] = u[:, bl * LANES:(bl + 1) * LANES]
            for t in range(SSM_CHUNK):
                rows = u_scr[b, pl.ds(t, chunks, stride=SSM_CHUNK), :]
                ucv_ref[b, :, t * LANES:(t + 1) * LANES] = rows.astype(BF16)

    def conv_pass(cb):
        cols = slice(cb * CONV_COLS, (cb + 1) * CONV_COLS)
        z_scr = z_scrs[cb]
        for r in range(1, SUBLANES):
            zs_scr[r - 1] = z_scr[pl.ds(r, shifted_rows), :]
        acc = None
        for k in range(CONV_K):
            q, r = divmod(CONV_HALO - (CONV_K - 1) + k, SUBLANES)
            if r == 0:
                src = z_scr[pl.ds(q * SUBLANES, tm), :]
            else:
                src = zs_scr[r - 1, pl.ds(q * SUBLANES, tm), :]
            term = cw_ref[k:k + 1, cols] * src
            acc = term if acc is None else acc + term
        co_scr[:, cols] = acc

    glu_pass(0)
    for cb in range(n_pass):
        if cb + 1 < n_pass:
            glu_pass(cb + 1)
        ssm_input_pass(cb)
        conv_pass(cb)

    co = co_scr[...]
    mu = jnp.mean(co, axis=-1, keepdims=True)
    xc = co - mu
    zn = xc * lax.rsqrt(jnp.mean(xc * xc, axis=-1, keepdims=True) + EPS)
    c_ref[...] = jax.nn.silu(zn * lng_ref[...] + lnb_ref[...]).astype(BF16)


def _in_proj(x2, gain, w_in, conv_w, ln_g, ln_b, *, d_ssm, d_conv, tm, seq):
    n_tok, d_model = x2.shape
    chunks = tm // SSM_CHUNK
    nb = d_ssm // LANES
    const = lambda i: (0, 0)
    return pl.pallas_call(
        functools.partial(_in_proj_kernel, d_ssm=d_ssm, d_conv=d_conv, chunks=chunks,
                          tiles_per_seq=seq // tm),
        grid=(n_tok // tm,),
        in_specs=[
            pl.BlockSpec((tm, d_model), lambda i: (i, 0)),
            pl.BlockSpec((1, d_model), const),
            pl.BlockSpec(w_in.shape, const),
            pl.BlockSpec(conv_w.shape, const),
            pl.BlockSpec(ln_g.shape, const),
            pl.BlockSpec(ln_b.shape, const),
        ],
        out_specs=[
            pl.BlockSpec((nb, chunks, SSM_CHUNK * LANES), lambda i: (0, i, 0)),
            pl.BlockSpec((tm, d_conv), lambda i: (i, 0)),
        ],
        out_shape=[
            jax.ShapeDtypeStruct((nb, n_tok // SSM_CHUNK, SSM_CHUNK * LANES), BF16),
            jax.ShapeDtypeStruct((n_tok, d_conv), BF16),
        ],
        scratch_shapes=[
            pltpu.VMEM((nb, tm, LANES), F32),
            pltpu.VMEM((SUBLANES - 1, tm + CONV_HALO - SUBLANES, CONV_COLS), F32),
            pltpu.VMEM((tm, d_conv), F32),
        ] + [pltpu.VMEM((tm + CONV_HALO, CONV_COLS), F32)] * (d_conv // CONV_COLS),
        compiler_params=pltpu.CompilerParams(
            dimension_semantics=("arbitrary",), vmem_limit_bytes=VMEM_LIMIT),
        name="in_proj",
    )(x2, gain, w_in, conv_w, ln_g, ln_b)


def _ssm_params(log_dt, lam_re, lam_im, b_re, b_im, c_re, c_im, d):
    n_groups = lam_re.shape[0]
    nb = n_groups // GROUPS_PER_BLOCK
    half = STATE_W // 2
    lam = lax.complex(lam_re.astype(F32), lam_im.astype(F32))
    lam_dt = lam * jnp.exp(log_dt.astype(F32))[:, None]
    steps = jnp.arange(SSM_CHUNK + 1, dtype=F32)[:, None, None]
    pw = jnp.exp(lam_dt[None] * steps)
    b_bar = ((pw[1] - 1.0) / lam)[..., None] * lax.complex(b_re.astype(F32), b_im.astype(F32))
    b_t = b_bar.reshape(nb, GROUPS_PER_BLOCK, SSM_STATE, SSM_GROUP).transpose(0, 3, 1, 2)
    b_t = b_t.reshape(nb, SSM_GROUP, half)
    c_t = lax.complex(c_re.astype(F32), c_im.astype(F32))
    c_t = c_t.reshape(nb, GROUPS_PER_BLOCK, SSM_GROUP, SSM_STATE).transpose(0, 2, 1, 3)
    c_t = c_t.reshape(nb, SSM_GROUP, half)
    bc = jnp.stack([b_t.real, b_t.imag, c_t.real, c_t.imag], axis=1)
    pw_t = pw.reshape(SSM_CHUNK + 1, nb, half).transpose(1, 0, 2)
    pw_t = jnp.stack([pw_t.real, pw_t.imag], axis=1)
    drow = d.astype(F32).reshape(nb, 1, LANES)

    row = jnp.arange(SUBLANES)
    expo = jnp.stack([jnp.where(row >= 1, 1, 0), jnp.where(row >= 2, 2, 0),
                      jnp.where(row >= 4, 4, 0), row + 1])
    keep = jnp.stack([row >= 1, row >= 2, row >= 4, row >= 0])
    ap = jnp.exp(lam_dt[None, None] * (SSM_CHUNK * expo.astype(F32))[:, :, None, None])
    ap = jnp.where(keep[:, :, None, None], ap, 0.0)
    ap = jnp.stack([ap.real, ap.imag], axis=2)
    ap = ap.reshape(4, SUBLANES, 2, nb, half)
    atab = ap.transpose(3, 0, 1, 2, 4).reshape(nb, 4, SUBLANES, STATE_W)
    return pw_t, bc, drow, atab


def _toeplitz_rows(k):
    return (k // (MXU_COLS // LANES) + 1) * MXU_COLS


def _ssm_kernel(u_ref, pw_ref, bc_ref, d_ref, at_ref, y_ref, w1_scr, qct_scr, b_scr, s_scr):
    cw = SSM_CHUNK * LANES
    half = STATE_W // 2
    n_chunks = u_ref.shape[0]

    @pl.when(pl.program_id(1) == 0)
    def _build_operators():
        row_g = lax.broadcasted_iota(jnp.int32, (LANES, STATE_W), 0) // SSM_GROUP
        col_g = (lax.broadcasted_iota(jnp.int32, (LANES, STATE_W), 1) % half) // SSM_STATE
        smask = row_g == col_g
        b_r, b_i, c_r, c_i = bc_ref[0], bc_ref[1], bc_ref[2], bc_ref[3]

        def slab(re, im):
            v = jnp.concatenate([jnp.concatenate([re, im], axis=1)] * GROUPS_PER_BLOCK, axis=0)
            return jnp.where(smask, v, 0.0).astype(BF16)

        for m in range(SSM_CHUNK):
            rows = slice(m * LANES, (m + 1) * LANES)
            e = SSM_CHUNK - 1 - m
            p_r, p_i = pw_ref[0, e:e + 1, :], pw_ref[1, e:e + 1, :]
            w1_scr[rows, cw:] = slab(p_r * b_r - p_i * b_i, p_r * b_i + p_i * b_r)
            q_r, q_i = pw_ref[0, m + 1:m + 2, :], pw_ref[1, m + 1:m + 2, :]
            qct_scr[rows, :] = slab(c_r * q_r - c_i * q_i, -(c_r * q_i + c_i * q_r))

        c0 = slab(c_r, -c_i)
        diag = (lax.broadcasted_iota(jnp.int32, (LANES, LANES), 0)
                == lax.broadcasted_iota(jnp.int32, (LANES, LANES), 1))
        blocks = []
        for m in range(SSM_CHUNK):
            e = SSM_CHUNK - 1 - m
            tap = lax.dot_general(w1_scr[e * LANES:(e + 1) * LANES, cw:], c0,
                                  (((1,), (1,)), ((), ())), preferred_element_type=F32)
            if m == 0:
                tap = tap + jnp.where(diag, d_ref[...], 0.0)
            blocks.append(tap.astype(BF16))
        zero_blk = jnp.zeros((LANES, LANES), BF16)
        for k in range(SSM_CHUNK):
            for j in range(_toeplitz_rows(k) // LANES):
                w1_scr[j * LANES:(j + 1) * LANES, k * LANES:(k + 1) * LANES] = (
                    blocks[k - j] if k >= j else zero_blk)

    intra = []
    for n in range(cw // MXU_COLS):
        kdim = _toeplitz_rows(n * (MXU_COLS // LANES))
        intra.append(jnp.dot(u_ref[:, 0:kdim], w1_scr[0:kdim, n * MXU_COLS:(n + 1) * MXU_COLS],
                             preferred_element_type=F32))
    bc = jnp.dot(u_ref[...], w1_scr[:, cw:], preferred_element_type=F32)
    bc = pltpu.roll(bc, 1, axis=0)
    first_row = lax.broadcasted_iota(jnp.int32, bc.shape, 0) == 0
    b_scr[...] = jnp.where(first_row, 0.0, bc)

    a1, a2, a4, ap = at_ref[0], at_ref[1], at_ref[2], at_ref[3]

    def cmul_add(xr, xi, ar, ai, sr, si):
        return xr + ar * sr - ai * si, xi + ar * si + ai * sr

    def block_scan(i, carry):
        cr, ci = carry
        r0 = pl.multiple_of(i * SUBLANES, SUBLANES)
        x = b_scr[pl.ds(r0, SUBLANES), :]
        xr, xi = x[:, :half], x[:, half:]
        for d, tab in ((1, a1), (2, a2), (4, a4)):
            sr = pltpu.roll(xr, d, axis=0)
            si = pltpu.roll(xi, d, axis=0)
            xr, xi = cmul_add(xr, xi, tab[:, :half], tab[:, half:], sr, si)
        xr, xi = cmul_add(xr, xi, ap[:, :half], ap[:, half:], cr, ci)
        s_scr[pl.ds(r0, SUBLANES), :] = jnp.concatenate([xr, xi], axis=1)
        return xr[SUBLANES - 1:, :], xi[SUBLANES - 1:, :]

    zero = jnp.zeros((1, half), F32)
    lax.fori_loop(0, n_chunks // SUBLANES, block_scan, (zero, zero))

    y = jnp.concatenate(intra, axis=1) + lax.dot_general(
        s_scr[...].astype(BF16), qct_scr[...], (((1,), (1,)), ((), ())),
        preferred_element_type=F32)
    y_ref[...] = y.astype(BF16)


def _ssm(ucv, pw_t, bc, drow, atab, *, batch):
    nb, n_chunks, cw = ucv.shape
    cps = n_chunks // batch
    return pl.pallas_call(
        _ssm_kernel,
        grid=(nb, batch),
        in_specs=[
            pl.BlockSpec((None, cps, cw), lambda b, s: (b, s, 0)),
            pl.BlockSpec((None,) + pw_t.shape[1:], lambda b, s: (b, 0, 0, 0)),
            pl.BlockSpec((None,) + bc.shape[1:], lambda b, s: (b, 0, 0, 0)),
            pl.BlockSpec((None,) + drow.shape[1:], lambda b, s: (b, 0, 0)),
            pl.BlockSpec((None,) + atab.shape[1:], lambda b, s: (b, 0, 0, 0)),
        ],
        out_specs=pl.BlockSpec((None, cps, cw), lambda b, s: (b, s, 0)),
        out_shape=jax.ShapeDtypeStruct(ucv.shape, BF16),
        scratch_shapes=[
            pltpu.VMEM((cw, cw + STATE_W), BF16),
            pltpu.VMEM((cw, STATE_W), BF16),
            pltpu.VMEM((cps, STATE_W), F32),
            pltpu.VMEM((cps, STATE_W), F32),
        ],
        compiler_params=pltpu.CompilerParams(
            dimension_semantics=("arbitrary", "arbitrary"), vmem_limit_bytes=VMEM_LIMIT),
        name="ssm",
    )(ucv, pw_t, bc, drow, atab)


def _mix_out_kernel(ycv_ref, c_ref, x_ref, wglu_ref, wout_ref, pg_ref, fg_ref, o_ref, h_ref,
                    y_scr):
    nb, chunks, _ = ycv_ref.shape
    d_ssm = nb * LANES

    for b in range(nb):
        for t in range(SSM_CHUNK):
            rows = ycv_ref[b, :, t * LANES:(t + 1) * LANES]
            y_scr[b, pl.ds(t, chunks, stride=SSM_CHUNK), :] = rows.astype(F32)
    y = jax.nn.gelu(jnp.concatenate([y_scr[b] for b in range(nb)], axis=1))
    gate = jnp.dot(y.astype(BF16), wglu_ref[...], preferred_element_type=F32)
    a = y * jax.nn.sigmoid(gate)

    yo = (jnp.dot(a.astype(BF16), wout_ref[0:d_ssm, :], preferred_element_type=F32)
          + jnp.dot(c_ref[...], wout_ref[d_ssm:, :], preferred_element_type=F32))
    x1 = x_ref[...] + _rms(yo, pg_ref[...])
    o_ref[...] = x1
    h_ref[...] = _rms(x1, fg_ref[...]).astype(BF16)


def _mix_out(ycv, c, x2, w_glu, w_out, post_g, pre_ffn_g, *, tm):
    n_tok, d_model = x2.shape
    nb = ycv.shape[0]
    d_conv = c.shape[1]
    chunks = tm // SSM_CHUNK
    const = lambda i: (0, 0)
    return pl.pallas_call(
        _mix_out_kernel,
        grid=(n_tok // tm,),
        in_specs=[
            pl.BlockSpec((nb, chunks, SSM_CHUNK * LANES), lambda i: (0, i, 0)),
            pl.BlockSpec((tm, d_conv), lambda i: (i, 0)),
            pl.BlockSpec((tm, d_model), lambda i: (i, 0)),
            pl.BlockSpec(w_glu.shape, const),
            pl.BlockSpec(w_out.shape, const),
            pl.BlockSpec(post_g.shape, const),
            pl.BlockSpec(pre_ffn_g.shape, const),
        ],
        out_specs=[
            pl.BlockSpec((tm, d_model), lambda i: (i, 0)),
            pl.BlockSpec((tm, d_model), lambda i: (i, 0)),
        ],
        out_shape=[
            jax.ShapeDtypeStruct((n_tok, d_model), F32),
            jax.ShapeDtypeStruct((n_tok, d_model), BF16),
        ],
        scratch_shapes=[pltpu.VMEM((nb, tm, LANES), F32)],
        compiler_params=pltpu.CompilerParams(
            dimension_semantics=("arbitrary",), vmem_limit_bytes=VMEM_LIMIT),
        name="mix_out",
    )(ycv, c, x2, w_glu, w_out, post_g, pre_ffn_g)


def _ffn_kernel(x_ref, h_ref, wg_ref, wv_ref, cg_ref, cv_ref, wd_ref, pg_ref, o_ref,
                acc_scr, ext_scr, carry_scr, *, tiles_per_seq):
    i = pl.program_id(0)
    j = pl.program_id(1)
    tm = x_ref.shape[0]

    @pl.when(j == 0)
    def _start_tile():
        acc_scr[...] = jnp.zeros_like(acc_scr)

    seq_start = (i % tiles_per_seq) == 0
    h = h_ref[...]

    def up_pass(w_ref, slot, cols):
        up = jnp.dot(h, w_ref[:, cols], preferred_element_type=F32)
        ext_scr[slot, 0:SUBLANES, cols] = jnp.where(seq_start, 0.0, carry_scr[j, slot, :, cols])
        ext_scr[slot, SUBLANES:, cols] = up
        carry_scr[j, slot, :, cols] = up[tm - SUBLANES:, :]

    def conv(cw_ref, slot, cols):
        out = None
        for k in range(FFN_CONV_K):
            shift = FFN_CONV_K - 1 - k
            term = cw_ref[k:k + 1, cols] * ext_scr[slot, pl.ds(SUBLANES - shift, tm), cols]
            out = term if out is None else out + term
        return out

    passes = [slice(c0, c0 + FFN_COLS) for c0 in range(0, wd_ref.shape[0], FFN_COLS)]
    for cols in passes:
        up_pass(wg_ref, 0, cols)
        up_pass(wv_ref, 1, cols)
    for cols in passes:
        hidden = jax.nn.gelu(conv(cg_ref, 0, cols)) * conv(cv_ref, 1, cols)
        acc_scr[...] += jnp.dot(hidden.astype(BF16), wd_ref[cols, :],
                                preferred_element_type=F32)

    @pl.when(j == pl.num_programs(1) - 1)
    def _finish_tile():
        o_ref[...] = x_ref[...] + _rms(acc_scr[...], pg_ref[...])


def _ffn(x1, h2, w_up, conv_w, w_down, post_g, *, tm, tf, seq):
    n_tok, d_model = x1.shape
    d_ff = w_down.shape[0]
    n_ff = d_ff // tf
    const = lambda i, j: (0, 0)
    return pl.pallas_call(
        functools.partial(_ffn_kernel, tiles_per_seq=seq // tm),
        grid=(n_tok // tm, n_ff),
        in_specs=[
            pl.BlockSpec((tm, d_model), lambda i, j: (i, 0)),
            pl.BlockSpec((tm, d_model), lambda i, j: (i, 0)),
            pl.BlockSpec((d_model, tf), lambda i, j: (0, j)),
            pl.BlockSpec((d_model, tf), lambda i, j: (0, j + n_ff)),
            pl.BlockSpec((FFN_CONV_K, tf), lambda i, j: (0, j)),
            pl.BlockSpec((FFN_CONV_K, tf), lambda i, j: (0, j + n_ff)),
            pl.BlockSpec((tf, d_model), lambda i, j: (j, 0)),
            pl.BlockSpec(post_g.shape, const),
        ],
        out_specs=pl.BlockSpec((tm, d_model), lambda i, j: (i, 0)),
        out_shape=jax.ShapeDtypeStruct((n_tok, d_model), F32),
        scratch_shapes=[
            pltpu.VMEM((tm, d_model), F32),
            pltpu.VMEM((2, tm + SUBLANES, tf), F32),
            pltpu.VMEM((n_ff, 2, SUBLANES, tf), F32),
        ],
        compiler_params=pltpu.CompilerParams(
            dimension_semantics=("arbitrary", "arbitrary"), vmem_limit_bytes=VMEM_LIMIT),
        name="ffn",
    )(x1, h2, w_up, w_up, conv_w, conv_w, w_down, post_g)


def _layer(x, pre_mix_g, w_in, log_dt, lam_re, lam_im, b_re, b_im, c_re, c_im, d, w_glu,
           conv_w, ln_g, ln_b, w_out, post_mix_g, pre_ffn_g, w_up, ffn_conv_w, w_down,
           post_ffn_g, *, tm, tf):
    batch, seq, d_model = x.shape
    d_ssm = w_glu.shape[0]
    d_conv = conv_w.shape[1]
    assert seq % tm == 0 and tm % (SSM_CHUNK * SUBLANES) == 0 and tm % CONV_HALO == 0
    assert d_ssm % LANES == 0 and (seq // SSM_CHUNK) % SUBLANES == 0
    assert d_conv % CONV_COLS == 0 and tf % FFN_COLS == 0
    assert d_ssm % (LANES * (d_conv // CONV_COLS)) == 0
    row = lambda v: v.reshape(1, -1).astype(F32)

    x2 = x.reshape(batch * seq, d_model)
    ucv, c = _in_proj(x2, row(pre_mix_g), w_in.astype(BF16), conv_w.astype(F32), row(ln_g),
                      row(ln_b), d_ssm=d_ssm, d_conv=d_conv, tm=tm, seq=seq)
    ssm_params = _ssm_params(log_dt, lam_re, lam_im, b_re, b_im, c_re, c_im, d)
    ycv = _ssm(ucv, *ssm_params, batch=batch)
    x1, h2 = _mix_out(ycv, c, x2, w_glu.astype(BF16), w_out.astype(BF16), row(post_mix_g),
                      row(pre_ffn_g), tm=tm)
    out = _ffn(x1, h2, w_up.astype(BF16), ffn_conv_w.astype(F32), w_down.astype(BF16),
               row(post_ffn_g), tm=tm, tf=tf, seq=seq)
    return out.reshape(batch, seq, d_model)


def kernel(x, pre_mix_g, w_in, ssm_log_dt, ssm_lam_re, ssm_lam_im, ssm_b_re, ssm_b_im,
           ssm_c_re, ssm_c_im, ssm_d, ssm_w_glu, conv_w, conv_ln_g, conv_ln_b, w_out,
           post_mix_g, pre_ffn_g, ffn_w_up, ffn_conv_w, ffn_w_down, post_ffn_g):
    for i in range(pre_mix_g.shape[0]):
        x = _layer(x, pre_mix_g[i], w_in[i], ssm_log_dt[i], ssm_lam_re[i], ssm_lam_im[i],
                   ssm_b_re[i], ssm_b_im[i], ssm_c_re[i], ssm_c_im[i], ssm_d[i], ssm_w_glu[i],
                   conv_w[i], conv_ln_g[i], conv_ln_b[i], w_out[i], post_mix_g[i],
                   pre_ffn_g[i], ffn_w_up[i], ffn_conv_w[i], ffn_w_down[i], post_ffn_g[i],
                   tm=512, tf=512)
    return x
```

```python
import functools

import jax
import jax.numpy as jnp
from jax import lax
from jax.experimental import pallas as pl
from jax.experimental.pallas import tpu as pltpu

F32 = jnp.float32
BF16 = jnp.bfloat16

EPS = 1e-6
SSM_GROUP = 16
SSM_STATE = 64
CONV_K = 31
FFN_CONV_K = 3

LANES = 128
SUBLANES = 8
BF16_SUBLANES = 16
SSM_CHUNK = 16
GROUPS_PER_BLOCK = LANES // SSM_GROUP
STATE_W = 2 * GROUPS_PER_BLOCK * SSM_STATE
CONV_HALO = 32
MXU_COLS = 256
CONV_COLS = MXU_COLS
FFN_COLS = MXU_COLS
VMEM_LIMIT = 56 * 1024 * 1024
MIX_PARTS = 2


def _rms(x, gain):
    return x * lax.rsqrt(jnp.mean(x * x, axis=-1, keepdims=True) + EPS) * gain


def _in_proj_kernel(x_ref, g_ref, w_ref, cw_ref, lng_ref, lnb_ref, *rest,
                    d_ssm, d_conv, chunks, tiles_per_seq, n_cast):
    cast_in, rest = rest[:n_cast], rest[n_cast:]
    (ucv_ref, c_ref), rest = rest[:2], rest[2:]
    cast_out, rest = rest[:n_cast], rest[n_cast:]
    (u_scr, zs_scr, co_scr), z_scrs = rest[:3], rest[3:]
    tm = x_ref.shape[0]
    i = pl.program_id(0)

    for src, dst in zip(cast_in, cast_out):
        dst[...] = src[...].astype(BF16)

    @pl.when(i == 0)
    def _init_context():
        for z_scr in z_scrs:
            z_scr[...] = jnp.zeros_like(z_scr)

    seq_start = (i % tiles_per_seq) == 0
    for z_scr in z_scrs:
        z_scr[0:CONV_HALO, :] = jnp.where(seq_start, 0.0, z_scr[tm:tm + CONV_HALO, :])

    h = _rms(x_ref[...], g_ref[...]).astype(BF16)
    shifted_rows = tm + CONV_HALO - SUBLANES
    n_pass = d_conv // CONV_COLS
    u_cols = d_ssm // n_pass

    def glu_pass(cb):
        v0 = d_ssm + cb * CONV_COLS
        g0 = d_ssm + d_conv + cb * CONV_COLS
        cv = jnp.dot(h, w_ref[:, v0:v0 + CONV_COLS], preferred_element_type=F32)
        cg = jnp.dot(h, w_ref[:, g0:g0 + CONV_COLS], preferred_element_type=F32)
        z_scrs[cb][CONV_HALO:, :] = cv * jax.nn.sigmoid(cg)

    def ssm_input_pass(cb):
        u = jnp.dot(h, w_ref[:, cb * u_cols:(cb + 1) * u_cols], preferred_element_type=F32)
        for bl in range(u_cols // LANES):
            b = cb * (u_cols // LANES) + bl
            u_scr[b] = u[:, bl * LANES:(bl + 1) * LANES]
            for t in range(SSM_CHUNK):
                rows = u_scr[b, pl.ds(t, chunks, stride=SSM_CHUNK), :]
                ucv_ref[b, :, t * LANES:(t + 1) * LANES] = rows.astype(BF16)

    def conv_pass(cb):
        cols = slice(cb * CONV_COLS, (cb + 1) * CONV_COLS)
        z_scr = z_scrs[cb]
        for r in range(1, SUBLANES):
            zs_scr[r - 1] = z_scr[pl.ds(r, shifted_rows), :]
        acc = None
        for k in range(CONV_K):
            q, r = divmod(CONV_HALO - (CONV_K - 1) + k, SUBLANES)
            if r == 0:
                src = z_scr[pl.ds(q * SUBLANES, tm), :]
            else:
                src = zs_scr[r - 1, pl.ds(q * SUBLANES, tm), :]
            term = cw_ref[k:k + 1, cols] * src
            acc = term if acc is None else acc + term
        co_scr[:, cols] = acc

    glu_pass(0)
    for cb in range(n_pass):
        if cb + 1 < n_pass:
            glu_pass(cb + 1)
        ssm_input_pass(cb)
        conv_pass(cb)

    co = co_scr[...]
    mu = jnp.mean(co, axis=-1, keepdims=True)
    xc = co - mu
    zn = xc * lax.rsqrt(jnp.mean(xc * xc, axis=-1, keepdims=True) + EPS)
    c_ref[...] = jax.nn.silu(zn * lng_ref[...] + lnb_ref[...]).astype(BF16)


def _in_proj(x2, gain, w_in, conv_w, ln_g, ln_b, to_cast, *, d_ssm, d_conv, tm, seq):
    n_tok, d_model = x2.shape
    chunks = tm // SSM_CHUNK
    nb = d_ssm // LANES
    steps = n_tok // tm
    const = lambda i: (0, 0)
    for w in to_cast:
        assert w.shape[0] % (steps * BF16_SUBLANES) == 0
    cast_specs = lambda: [pl.BlockSpec((w.shape[0] // steps, w.shape[1]), lambda i: (i, 0))
                          for w in to_cast]
    return pl.pallas_call(
        functools.partial(_in_proj_kernel, d_ssm=d_ssm, d_conv=d_conv, chunks=chunks,
                          tiles_per_seq=seq // tm, n_cast=len(to_cast)),
        grid=(steps,),
        in_specs=[
            pl.BlockSpec((tm, d_model), lambda i: (i, 0)),
            pl.BlockSpec((1, d_model), const),
            pl.BlockSpec(w_in.shape, const, pipeline_mode=pl.Buffered(1)),
            pl.BlockSpec(conv_w.shape, const),
            pl.BlockSpec(ln_g.shape, const),
            pl.BlockSpec(ln_b.shape, const),
        ] + cast_specs(),
        out_specs=[
            pl.BlockSpec((nb, chunks, SSM_CHUNK * LANES), lambda i: (0, i, 0)),
            pl.BlockSpec((tm, d_conv), lambda i: (i, 0)),
        ] + cast_specs(),
        out_shape=[
            jax.ShapeDtypeStruct((nb, n_tok // SSM_CHUNK, SSM_CHUNK * LANES), BF16),
            jax.ShapeDtypeStruct((n_tok, d_conv), BF16),
        ] + [jax.ShapeDtypeStruct(w.shape, BF16) for w in to_cast],
        scratch_shapes=[
            pltpu.VMEM((nb, tm, LANES), F32),
            pltpu.VMEM((SUBLANES - 1, tm + CONV_HALO - SUBLANES, CONV_COLS), F32),
            pltpu.VMEM((tm, d_conv), F32),
        ] + [pltpu.VMEM((tm + CONV_HALO, CONV_COLS), F32)] * (d_conv // CONV_COLS),
        compiler_params=pltpu.CompilerParams(
            dimension_semantics=("arbitrary",), vmem_limit_bytes=VMEM_LIMIT),
        name="in_proj",
    )(x2, gain, w_in, conv_w, ln_g, ln_b, *to_cast)


def _ssm_params(log_dt, lam_re, lam_im, b_re, b_im, c_re, c_im, d):
    n_groups = lam_re.shape[0]
    nb = n_groups // GROUPS_PER_BLOCK
    half = STATE_W // 2
    lam = lax.complex(lam_re.astype(F32), lam_im.astype(F32))
    lam_dt = lam * jnp.exp(log_dt.astype(F32))[:, None]
    steps = jnp.arange(SSM_CHUNK + 1, dtype=F32)[:, None, None]
    pw = jnp.exp(lam_dt[None] * steps)
    b_bar = ((pw[1] - 1.0) / lam)[..., None] * lax.complex(b_re.astype(F32), b_im.astype(F32))
    b_t = b_bar.reshape(nb, GROUPS_PER_BLOCK, SSM_STATE, SSM_GROUP).transpose(0, 3, 1, 2)
    b_t = b_t.reshape(nb, SSM_GROUP, half)
    c_t = lax.complex(c_re.astype(F32), c_im.astype(F32))
    c_t = c_t.reshape(nb, GROUPS_PER_BLOCK, SSM_GROUP, SSM_STATE).transpose(0, 2, 1, 3)
    c_t = c_t.reshape(nb, SSM_GROUP, half)
    bc = jnp.stack([b_t.real, b_t.imag, c_t.real, c_t.imag], axis=1)
    pw_t = pw.reshape(SSM_CHUNK + 1, nb, half).transpose(1, 0, 2)
    pw_t = jnp.stack([pw_t.real, pw_t.imag], axis=1)
    drow = d.astype(F32).reshape(nb, 1, LANES)

    row = jnp.arange(SUBLANES)
    expo = jnp.stack([jnp.where(row >= 1, 1, 0), jnp.where(row >= 2, 2, 0),
                      jnp.where(row >= 4, 4, 0), row + 1])
    keep = jnp.stack([row >= 1, row >= 2, row >= 4, row >= 0])
    ap = jnp.exp(lam_dt[None, None] * (SSM_CHUNK * expo.astype(F32))[:, :, None, None])
    ap = jnp.where(keep[:, :, None, None], ap, 0.0)
    ap = jnp.stack([ap.real, ap.imag], axis=2)
    ap = ap.reshape(4, SUBLANES, 2, nb, half)
    atab = ap.transpose(3, 0, 1, 2, 4).reshape(nb, 4, SUBLANES, STATE_W)
    return pw_t, bc, drow, atab


def _toeplitz_rows(k):
    return (k // (MXU_COLS // LANES) + 1) * MXU_COLS


def _ssm_kernel(u_ref, pw_ref, bc_ref, d_ref, at_ref, y_ref, w1_scr, qct_scr, b_scr, s_scr):
    cw = SSM_CHUNK * LANES
    half = STATE_W // 2
    n_chunks = u_ref.shape[0]

    @pl.when(pl.program_id(1) == 0)
    def _build_operators():
        row_g = lax.broadcasted_iota(jnp.int32, (LANES, STATE_W), 0) // SSM_GROUP
        col_g = (lax.broadcasted_iota(jnp.int32, (LANES, STATE_W), 1) % half) // SSM_STATE
        smask = row_g == col_g
        b_r, b_i, c_r, c_i = bc_ref[0], bc_ref[1], bc_ref[2], bc_ref[3]

        def slab(re, im):
            v = jnp.concatenate([jnp.concatenate([re, im], axis=1)] * GROUPS_PER_BLOCK, axis=0)
            return jnp.where(smask, v, 0.0).astype(BF16)

        for m in range(SSM_CHUNK):
            rows = slice(m * LANES, (m + 1) * LANES)
            e = SSM_CHUNK - 1 - m
            p_r, p_i = pw_ref[0, e:e + 1, :], pw_ref[1, e:e + 1, :]
            w1_scr[rows, cw:] = slab(p_r * b_r - p_i * b_i, p_r * b_i + p_i * b_r)
            q_r, q_i = pw_ref[0, m + 1:m + 2, :], pw_ref[1, m + 1:m + 2, :]
            qct_scr[rows, :] = slab(c_r * q_r - c_i * q_i, -(c_r * q_i + c_i * q_r))

        c0 = slab(c_r, -c_i)
        diag = (lax.broadcasted_iota(jnp.int32, (LANES, LANES), 0)
                == lax.broadcasted_iota(jnp.int32, (LANES, LANES), 1))
        blocks = []
        for m in range(SSM_CHUNK):
            e = SSM_CHUNK - 1 - m
            tap = lax.dot_general(w1_scr[e * LANES:(e + 1) * LANES, cw:], c0,
                                  (((1,), (1,)), ((), ())), preferred_element_type=F32)
            if m == 0:
                tap = tap + jnp.where(diag, d_ref[...], 0.0)
            blocks.append(tap.astype(BF16))
        zero_blk = jnp.zeros((LANES, LANES), BF16)
        for k in range(SSM_CHUNK):
            for j in range(_toeplitz_rows(k) // LANES):
                w1_scr[j * LANES:(j + 1) * LANES, k * LANES:(k + 1) * LANES] = (
                    blocks[k - j] if k >= j else zero_blk)

    intra = []
    for n in range(cw // MXU_COLS):
        kdim = _toeplitz_rows(n * (MXU_COLS // LANES))
        intra.append(jnp.dot(u_ref[:, 0:kdim], w1_scr[0:kdim, n * MXU_COLS:(n + 1) * MXU_COLS],
                             preferred_element_type=F32))
    bc = jnp.dot(u_ref[...], w1_scr[:, cw:], preferred_element_type=F32)
    bc = pltpu.roll(bc, 1, axis=0)
    first_row = lax.broadcasted_iota(jnp.int32, bc.shape, 0) == 0
    b_scr[...] = jnp.where(first_row, 0.0, bc)

    a1, a2, a4, ap = at_ref[0], at_ref[1], at_ref[2], at_ref[3]

    def cmul_add(xr, xi, ar, ai, sr, si):
        return xr + ar * sr - ai * si, xi + ar * si + ai * sr

    def block_scan(i, carry):
        cr, ci = carry
        r0 = pl.multiple_of(i * SUBLANES, SUBLANES)
        x = b_scr[pl.ds(r0, SUBLANES), :]
        xr, xi = x[:, :half], x[:, half:]
        for d, tab in ((1, a1), (2, a2), (4, a4)):
            sr = pltpu.roll(xr, d, axis=0)
            si = pltpu.roll(xi, d, axis=0)
            xr, xi = cmul_add(xr, xi, tab[:, :half], tab[:, half:], sr, si)
        xr, xi = cmul_add(xr, xi, ap[:, :half], ap[:, half:], cr, ci)
        s_scr[pl.ds(r0, SUBLANES), :] = jnp.concatenate([xr, xi], axis=1)
        return xr[SUBLANES - 1:, :], xi[SUBLANES - 1:, :]

    zero = jnp.zeros((1, half), F32)
    lax.fori_loop(0, n_chunks // SUBLANES, block_scan, (zero, zero))

    y = jnp.concatenate(intra, axis=1) + lax.dot_general(
        s_scr[...].astype(BF16), qct_scr[...], (((1,), (1,)), ((), ())),
        preferred_element_type=F32)
    y_ref[...] = y.astype(BF16)


def _ssm(ucv, pw_t, bc, drow, atab, *, batch):
    nb, n_chunks, cw = ucv.shape
    cps = n_chunks // batch
    return pl.pallas_call(
        _ssm_kernel,
        grid=(nb, batch),
        in_specs=[
            pl.BlockSpec((None, cps, cw), lambda b, s: (b, s, 0)),
            pl.BlockSpec((None,) + pw_t.shape[1:], lambda b, s: (b, 0, 0, 0)),
            pl.BlockSpec((None,) + bc.shape[1:], lambda b, s: (b, 0, 0, 0)),
            pl.BlockSpec((None,) + drow.shape[1:], lambda b, s: (b, 0, 0)),
            pl.BlockSpec((None,) + atab.shape[1:], lambda b, s: (b, 0, 0, 0)),
        ],
        out_specs=pl.BlockSpec((None, cps, cw), lambda b, s: (b, s, 0)),
        out_shape=jax.ShapeDtypeStruct(ucv.shape, BF16),
        scratch_shapes=[
            pltpu.VMEM((cw, cw + STATE_W), BF16),
            pltpu.VMEM((cw, STATE_W), BF16),
            pltpu.VMEM((cps, STATE_W), F32),
            pltpu.VMEM((cps, STATE_W), F32),
        ],
        compiler_params=pltpu.CompilerParams(
            dimension_semantics=("arbitrary", "arbitrary"), vmem_limit_bytes=VMEM_LIMIT),
        name="ssm",
    )(ucv, pw_t, bc, drow, atab)


def _mix_out_kernel(ycv_ref, c_ref, x_ref, wglu_ref, wout_ref, pg_ref, fg_ref, o_ref, h_ref,
                    *y_scrs):
    nb, chunks, _ = ycv_ref.shape
    d_ssm = nb * LANES
    n_part = len(y_scrs)
    pc = chunks // n_part
    pr = pc * SSM_CHUNK
    for part, y_scr in enumerate(y_scrs):
        rows = slice(part * pr, (part + 1) * pr)
        for b in range(nb):
            for t in range(SSM_CHUNK):
                piece = ycv_ref[b, part * pc:(part + 1) * pc, t * LANES:(t + 1) * LANES]
                y_scr[b, pl.ds(t, pc, stride=SSM_CHUNK), :] = piece.astype(F32)
        y = jax.nn.gelu(jnp.concatenate([y_scr[b] for b in range(nb)], axis=1))
        gate = jnp.dot(y.astype(BF16), wglu_ref[...], preferred_element_type=F32)
        a = y * jax.nn.sigmoid(gate)

        yo = (jnp.dot(a.astype(BF16), wout_ref[0:d_ssm, :], preferred_element_type=F32)
              + jnp.dot(c_ref[rows, :], wout_ref[d_ssm:, :], preferred_element_type=F32))
        x1 = x_ref[rows, :] + _rms(yo, pg_ref[...])
        o_ref[rows, :] = x1
        h_ref[rows, :] = _rms(x1, fg_ref[...]).astype(BF16)


def _mix_out(ycv, c, x2, w_glu, w_out, post_g, pre_ffn_g, *, tm):
    n_tok, d_model = x2.shape
    nb = ycv.shape[0]
    d_conv = c.shape[1]
    chunks = tm // SSM_CHUNK
    const = lambda i: (0, 0)
    return pl.pallas_call(
        _mix_out_kernel,
        grid=(n_tok // tm,),
        in_specs=[
            pl.BlockSpec((nb, chunks, SSM_CHUNK * LANES), lambda i: (0, i, 0)),
            pl.BlockSpec((tm, d_conv), lambda i: (i, 0)),
            pl.BlockSpec((tm, d_model), lambda i: (i, 0)),
            pl.BlockSpec(w_glu.shape, const),
            pl.BlockSpec(w_out.shape, const),
            pl.BlockSpec(post_g.shape, const),
            pl.BlockSpec(pre_ffn_g.shape, const),
        ],
        out_specs=[
            pl.BlockSpec((tm, d_model), lambda i: (i, 0)),
            pl.BlockSpec((tm, d_model), lambda i: (i, 0)),
        ],
        out_shape=[
            jax.ShapeDtypeStruct((n_tok, d_model), F32),
            jax.ShapeDtypeStruct((n_tok, d_model), BF16),
        ],
        scratch_shapes=[pltpu.VMEM((nb, tm // MIX_PARTS, LANES), F32)] * MIX_PARTS,
        compiler_params=pltpu.CompilerParams(
            dimension_semantics=("arbitrary",), vmem_limit_bytes=VMEM_LIMIT),
        name="mix_out",
    )(ycv, c, x2, w_glu, w_out, post_g, pre_ffn_g)


def _ffn_kernel(x_ref, h_ref, wg_ref, wv_ref, cg_ref, cv_ref, wd_ref, pg_ref, o_ref,
                acc_scr, ext_scr, carry_scr, *, tiles_per_seq):
    i = pl.program_id(0)
    j = pl.program_id(1)
    tm = x_ref.shape[0]

    @pl.when(j == 0)
    def _start_tile():
        acc_scr[...] = jnp.zeros_like(acc_scr)

    seq_start = (i % tiles_per_seq) == 0
    h = h_ref[...]

    def up_pass(w_ref, slot, cols):
        up = jnp.dot(h, w_ref[:, cols], preferred_element_type=F32)
        ext_scr[slot, 0:SUBLANES, cols] = jnp.where(seq_start, 0.0, carry_scr[j, slot, :, cols])
        ext_scr[slot, SUBLANES:, cols] = up
        carry_scr[j, slot, :, cols] = up[tm - SUBLANES:, :]

    def conv(cw_ref, slot, cols):
        out = None
        for k in range(FFN_CONV_K):
            shift = FFN_CONV_K - 1 - k
            term = cw_ref[k:k + 1, cols] * ext_scr[slot, pl.ds(SUBLANES - shift, tm), cols]
            out = term if out is None else out + term
        return out

    passes = [slice(c0, c0 + FFN_COLS) for c0 in range(0, wd_ref.shape[0], FFN_COLS)]
    for cols in passes:
        up_pass(wg_ref, 0, cols)
        up_pass(wv_ref, 1, cols)
    for cols in passes:
        hidden = jax.nn.gelu(conv(cg_ref, 0, cols)) * conv(cv_ref, 1, cols)
        acc_scr[...] += jnp.dot(hidden.astype(BF16), wd_ref[cols, :],
                                preferred_element_type=F32)

    @pl.when(j == pl.num_programs(1) - 1)
    def _finish_tile():
        o_ref[...] = x_ref[...] + _rms(acc_scr[...], pg_ref[...])


def _ffn(x1, h2, w_up, conv_w, w_down, post_g, *, tm, tf, seq):
    n_tok, d_model = x1.shape
    d_ff = w_down.shape[0]
    n_ff = d_ff // tf
    const = lambda i, j: (0, 0)
    return pl.pallas_call(
        functools.partial(_ffn_kernel, tiles_per_seq=seq // tm),
        grid=(n_tok // tm, n_ff),
        in_specs=[
            pl.BlockSpec((tm, d_model), lambda i, j: (i, 0)),
            pl.BlockSpec((tm, d_model), lambda i, j: (i, 0)),
            pl.BlockSpec((d_model, tf), lambda i, j: (0, j)),
            pl.BlockSpec((d_model, tf), lambda i, j: (0, j + n_ff)),
            pl.BlockSpec((FFN_CONV_K, tf), lambda i, j: (0, j)),
            pl.BlockSpec((FFN_CONV_K, tf), lambda i, j: (0, j + n_ff)),
            pl.BlockSpec((tf, d_model), lambda i, j: (j, 0)),
            pl.BlockSpec(post_g.shape, const),
        ],
        out_specs=pl.BlockSpec((tm, d_model), lambda i, j: (i, 0)),
        out_shape=jax.ShapeDtypeStruct((n_tok, d_model), F32),
        scratch_shapes=[
            pltpu.VMEM((tm, d_model), F32),
            pltpu.VMEM((2, tm + SUBLANES, tf), F32),
            pltpu.VMEM((n_ff, 2, SUBLANES, tf), F32),
        ],
        compiler_params=pltpu.CompilerParams(
            dimension_semantics=("arbitrary", "arbitrary"), vmem_limit_bytes=VMEM_LIMIT),
        name="ffn",
    )(x1, h2, w_up, w_up, conv_w, conv_w, w_down, post_g)


def _layer(x, pre_mix_g, w_in, log_dt, lam_re, lam_im, b_re, b_im, c_re, c_im, d, w_glu,
           conv_w, ln_g, ln_b, w_out, post_mix_g, pre_ffn_g, w_up, ffn_conv_w, w_down,
           post_ffn_g, *, tm, tf):
    batch, seq, d_model = x.shape
    d_ssm = w_glu.shape[0]
    d_conv = conv_w.shape[1]
    assert seq % tm == 0 and tm % (SSM_CHUNK * SUBLANES) == 0 and tm % CONV_HALO == 0
    assert d_ssm % LANES == 0 and (seq // SSM_CHUNK) % SUBLANES == 0
    assert d_conv % CONV_COLS == 0 and tf % FFN_COLS == 0
    assert d_ssm % (LANES * (d_conv // CONV_COLS)) == 0
    row = lambda v: v.reshape(1, -1).astype(F32)

    x2 = x.reshape(batch * seq, d_model)
    ucv, c, w_glu16, w_out16, w_up16, w_down16 = _in_proj(
        x2, row(pre_mix_g), w_in.astype(BF16), conv_w.astype(F32), row(ln_g), row(ln_b),
        [w.astype(F32) for w in (w_glu, w_out, w_up, w_down)],
        d_ssm=d_ssm, d_conv=d_conv, tm=tm, seq=seq)
    ssm_params = _ssm_params(log_dt, lam_re, lam_im, b_re, b_im, c_re, c_im, d)
    ycv = _ssm(ucv, *ssm_params, batch=batch)
    x1, h2 = _mix_out(ycv, c, x2, w_glu16, w_out16, row(post_mix_g), row(pre_ffn_g), tm=tm)
    out = _ffn(x1, h2, w_up16, ffn_conv_w.astype(F32), w_down16, row(post_ffn_g),
               tm=tm, tf=tf, seq=seq)
    return out.reshape(batch, seq, d_model)


def kernel(x, pre_mix_g, w_in, ssm_log_dt, ssm_lam_re, ssm_lam_im, ssm_b_re, ssm_b_im,
           ssm_c_re, ssm_c_im, ssm_d, ssm_w_glu, conv_w, conv_ln_g, conv_ln_b, w_out,
           post_mix_g, pre_ffn_g, ffn_w_up, ffn_conv_w, ffn_w_down, post_ffn_g):
    for i in range(pre_mix_g.shape[0]):
        x = _layer(x, pre_mix_g[i], w_in[i], ssm_log_dt[i], ssm_lam_re[i], ssm_lam_im[i],
                   ssm_b_re[i], ssm_b_im[i], ssm_c_re[i], ssm_c_im[i], ssm_d[i], ssm_w_glu[i],
                   conv_w[i], conv_ln_g[i], conv_ln_b[i], w_out[i], post_mix_g[i],
                   pre_ffn_g[i], ffn_w_up[i], ffn_conv_w[i], ffn_w_down[i], post_ffn_g[i],
                   tm=512, tf=512)
    return x
```

```python
import functools

import jax
import jax.numpy as jnp
from jax import lax
from jax.experimental import pallas as pl
from jax.experimental.pallas import tpu as pltpu

F32 = jnp.float32
BF16 = jnp.bfloat16

EPS = 1e-6
SSM_GROUP = 16
SSM_STATE = 64
CONV_K = 31
FFN_CONV_K = 3

LANES = 128
SUBLANES = 8
BF16_SUBLANES = 16
SSM_CHUNK = 16
GROUPS_PER_BLOCK = LANES // SSM_GROUP
STATE_W = 2 * GROUPS_PER_BLOCK * SSM_STATE
CONV_HALO = 32
MXU_COLS = 256
CONV_COLS = MXU_COLS
FFN_COLS = MXU_COLS
VMEM_LIMIT = 56 * 1024 * 1024
MIX_PARTS = 2


def _rms(x, gain):
    return x * lax.rsqrt(jnp.mean(x * x, axis=-1, keepdims=True) + EPS) * gain


def _in_proj_kernel(x_ref, g_ref, w_ref, cw_ref, lng_ref, lnb_ref, *rest,
                    d_ssm, d_conv, tiles_per_seq, n_cast):
    cast_in, rest = rest[:n_cast], rest[n_cast:]
    (u_ref, c_ref), rest = rest[:2], rest[2:]
    cast_out, rest = rest[:n_cast], rest[n_cast:]
    (zs_scr, co_scr), z_scrs = rest[:2], rest[2:]
    tm = x_ref.shape[0]
    i = pl.program_id(0)

    for src, dst in zip(cast_in, cast_out):
        dst[...] = src[...].astype(BF16)

    @pl.when(i == 0)
    def _init_context():
        for z_scr in z_scrs:
            z_scr[...] = jnp.zeros_like(z_scr)

    seq_start = (i % tiles_per_seq) == 0
    for z_scr in z_scrs:
        z_scr[0:CONV_HALO, :] = jnp.where(seq_start, 0.0, z_scr[tm:tm + CONV_HALO, :])

    h = _rms(x_ref[...], g_ref[...]).astype(BF16)
    shifted_rows = tm + CONV_HALO - SUBLANES
    n_pass = d_conv // CONV_COLS
    u_cols = d_ssm // n_pass

    def glu_pass(cb):
        v0 = d_ssm + cb * CONV_COLS
        g0 = d_ssm + d_conv + cb * CONV_COLS
        cv = jnp.dot(h, w_ref[:, v0:v0 + CONV_COLS], preferred_element_type=F32)
        cg = jnp.dot(h, w_ref[:, g0:g0 + CONV_COLS], preferred_element_type=F32)
        z_scrs[cb][CONV_HALO:, :] = cv * jax.nn.sigmoid(cg)

    def ssm_input_pass(cb):
        cols = slice(cb * u_cols, (cb + 1) * u_cols)
        u_ref[:, cols] = jnp.dot(h, w_ref[:, cols], preferred_element_type=F32)

    def conv_pass(cb):
        cols = slice(cb * CONV_COLS, (cb + 1) * CONV_COLS)
        z_scr = z_scrs[cb]
        for r in range(1, SUBLANES):
            zs_scr[r - 1] = z_scr[pl.ds(r, shifted_rows), :]
        acc = None
        for k in range(CONV_K):
            q, r = divmod(CONV_HALO - (CONV_K - 1) + k, SUBLANES)
            if r == 0:
                src = z_scr[pl.ds(q * SUBLANES, tm), :]
            else:
                src = zs_scr[r - 1, pl.ds(q * SUBLANES, tm), :]
            term = cw_ref[k:k + 1, cols] * src
            acc = term if acc is None else acc + term
        co_scr[:, cols] = acc

    glu_pass(0)
    for cb in range(n_pass):
        if cb + 1 < n_pass:
            glu_pass(cb + 1)
        ssm_input_pass(cb)
        conv_pass(cb)

    co = co_scr[...]
    mu = jnp.mean(co, axis=-1, keepdims=True)
    xc = co - mu
    zn = xc * lax.rsqrt(jnp.mean(xc * xc, axis=-1, keepdims=True) + EPS)
    c_ref[...] = jax.nn.silu(zn * lng_ref[...] + lnb_ref[...]).astype(BF16)


def _in_proj(x2, gain, w_in, conv_w, ln_g, ln_b, to_cast, *, d_ssm, d_conv, tm, seq):
    n_tok, d_model = x2.shape
    steps = n_tok // tm
    const = lambda i: (0, 0)
    for w in to_cast:
        assert w.shape[0] % (steps * BF16_SUBLANES) == 0
    cast_specs = lambda: [pl.BlockSpec((w.shape[0] // steps, w.shape[1]), lambda i: (i, 0))
                          for w in to_cast]
    return pl.pallas_call(
        functools.partial(_in_proj_kernel, d_ssm=d_ssm, d_conv=d_conv,
                          tiles_per_seq=seq // tm, n_cast=len(to_cast)),
        grid=(steps,),
        in_specs=[
            pl.BlockSpec((tm, d_model), lambda i: (i, 0)),
            pl.BlockSpec((1, d_model), const),
            pl.BlockSpec(w_in.shape, const, pipeline_mode=pl.Buffered(1)),
            pl.BlockSpec(conv_w.shape, const),
            pl.BlockSpec(ln_g.shape, const),
            pl.BlockSpec(ln_b.shape, const),
        ] + cast_specs(),
        out_specs=[
            pl.BlockSpec((tm, d_ssm), lambda i: (i, 0)),
            pl.BlockSpec((tm, d_conv), lambda i: (i, 0)),
        ] + cast_specs(),
        out_shape=[
            jax.ShapeDtypeStruct((n_tok, d_ssm), F32),
            jax.ShapeDtypeStruct((n_tok, d_conv), BF16),
        ] + [jax.ShapeDtypeStruct(w.shape, BF16) for w in to_cast],
        scratch_shapes=[
            pltpu.VMEM((SUBLANES - 1, tm + CONV_HALO - SUBLANES, CONV_COLS), F32),
            pltpu.VMEM((tm, d_conv), F32),
        ] + [pltpu.VMEM((tm + CONV_HALO, CONV_COLS), F32)] * (d_conv // CONV_COLS),
        compiler_params=pltpu.CompilerParams(
            dimension_semantics=("arbitrary",), vmem_limit_bytes=VMEM_LIMIT),
        name="in_proj",
    )(x2, gain, w_in, conv_w, ln_g, ln_b, *to_cast)


def _ssm_params(log_dt, lam_re, lam_im, b_re, b_im, c_re, c_im, d):
    n_groups = lam_re.shape[0]
    nb = n_groups // GROUPS_PER_BLOCK
    half = STATE_W // 2
    lam = lax.complex(lam_re.astype(F32), lam_im.astype(F32))
    lam_dt = lam * jnp.exp(log_dt.astype(F32))[:, None]
    steps = jnp.arange(SSM_CHUNK + 1, dtype=F32)[:, None, None]
    pw = jnp.exp(lam_dt[None] * steps)
    b_bar = ((pw[1] - 1.0) / lam)[..., None] * lax.complex(b_re.astype(F32), b_im.astype(F32))
    b_t = b_bar.reshape(nb, GROUPS_PER_BLOCK, SSM_STATE, SSM_GROUP).transpose(0, 3, 1, 2)
    b_t = b_t.reshape(nb, SSM_GROUP, half)
    c_t = lax.complex(c_re.astype(F32), c_im.astype(F32))
    c_t = c_t.reshape(nb, GROUPS_PER_BLOCK, SSM_GROUP, SSM_STATE).transpose(0, 2, 1, 3)
    c_t = c_t.reshape(nb, SSM_GROUP, half)
    bc = jnp.stack([b_t.real, b_t.imag, c_t.real, c_t.imag], axis=1)
    pw_t = pw.reshape(SSM_CHUNK + 1, nb, half).transpose(1, 0, 2)
    pw_t = jnp.stack([pw_t.real, pw_t.imag], axis=1)
    drow = d.astype(F32).reshape(nb, 1, LANES)

    row = jnp.arange(SUBLANES)
    expo = jnp.stack([jnp.where(row >= 1, 1, 0), jnp.where(row >= 2, 2, 0),
                      jnp.where(row >= 4, 4, 0), row + 1])
    keep = jnp.stack([row >= 1, row >= 2, row >= 4, row >= 0])
    ap = jnp.exp(lam_dt[None, None] * (SSM_CHUNK * expo.astype(F32))[:, :, None, None])
    ap = jnp.where(keep[:, :, None, None], ap, 0.0)
    ap = jnp.stack([ap.real, ap.imag], axis=2)
    ap = ap.reshape(4, SUBLANES, 2, nb, half)
    atab = ap.transpose(3, 0, 1, 2, 4).reshape(nb, 4, SUBLANES, STATE_W)
    return pw_t, bc, drow, atab


def _toeplitz_rows(k):
    return (k // (MXU_COLS // LANES) + 1) * MXU_COLS


def _ssm_kernel(u_ref, pw_ref, bc_ref, d_ref, at_ref, y_ref, w1_scr, qct_scr, b_scr, s_scr,
                ucv_scr):
    cw = SSM_CHUNK * LANES
    half = STATE_W // 2
    n_chunks = u_ref.shape[0] // SSM_CHUNK

    @pl.when(pl.program_id(1) == 0)
    def _build_operators():
        row_g = lax.broadcasted_iota(jnp.int32, (LANES, STATE_W), 0) // SSM_GROUP
        col_g = (lax.broadcasted_iota(jnp.int32, (LANES, STATE_W), 1) % half) // SSM_STATE
        smask = row_g == col_g
        b_r, b_i, c_r, c_i = bc_ref[0], bc_ref[1], bc_ref[2], bc_ref[3]

        def slab(re, im):
            v = jnp.concatenate([jnp.concatenate([re, im], axis=1)] * GROUPS_PER_BLOCK, axis=0)
            return jnp.where(smask, v, 0.0).astype(BF16)

        for m in range(SSM_CHUNK):
            rows = slice(m * LANES, (m + 1) * LANES)
            e = SSM_CHUNK - 1 - m
            p_r, p_i = pw_ref[0, e:e + 1, :], pw_ref[1, e:e + 1, :]
            w1_scr[rows, cw:] = slab(p_r * b_r - p_i * b_i, p_r * b_i + p_i * b_r)
            q_r, q_i = pw_ref[0, m + 1:m + 2, :], pw_ref[1, m + 1:m + 2, :]
            qct_scr[rows, :] = slab(c_r * q_r - c_i * q_i, -(c_r * q_i + c_i * q_r))

        c0 = slab(c_r, -c_i)
        diag = (lax.broadcasted_iota(jnp.int32, (LANES, LANES), 0)
                == lax.broadcasted_iota(jnp.int32, (LANES, LANES), 1))
        blocks = []
        for m in range(SSM_CHUNK):
            e = SSM_CHUNK - 1 - m
            tap = lax.dot_general(w1_scr[e * LANES:(e + 1) * LANES, cw:], c0,
                                  (((1,), (1,)), ((), ())), preferred_element_type=F32)
            if m == 0:
                tap = tap + jnp.where(diag, d_ref[...], 0.0)
            blocks.append(tap.astype(BF16))
        zero_blk = jnp.zeros((LANES, LANES), BF16)
        for k in range(SSM_CHUNK):
            for j in range(_toeplitz_rows(k) // LANES):
                w1_scr[j * LANES:(j + 1) * LANES, k * LANES:(k + 1) * LANES] = (
                    blocks[k - j] if k >= j else zero_blk)

    per_tile = MXU_COLS // LANES
    bc = None
    for n in range(cw // MXU_COLS):
        tile = slice(n * MXU_COLS, (n + 1) * MXU_COLS)
        for tl in range(per_tile):
            t = n * per_tile + tl
            ucv_scr[:, t * LANES:(t + 1) * LANES] = (
                u_ref[pl.ds(t, n_chunks, stride=SSM_CHUNK), :].astype(BF16))
        part = jnp.dot(ucv_scr[:, tile], w1_scr[tile, cw:], preferred_element_type=F32)
        bc = part if bc is None else bc + part
    bc = pltpu.roll(bc, 1, axis=0)
    first_row = lax.broadcasted_iota(jnp.int32, bc.shape, 0) == 0
    b_scr[...] = jnp.where(first_row, 0.0, bc)

    a1, a2, a4, ap = at_ref[0], at_ref[1], at_ref[2], at_ref[3]

    def cmul_add(xr, xi, ar, ai, sr, si):
        return xr + ar * sr - ai * si, xi + ar * si + ai * sr

    def block_scan(i, carry):
        cr, ci = carry
        r0 = pl.multiple_of(i * SUBLANES, SUBLANES)
        x = b_scr[pl.ds(r0, SUBLANES), :]
        xr, xi = x[:, :half], x[:, half:]
        for d, tab in ((1, a1), (2, a2), (4, a4)):
            sr = pltpu.roll(xr, d, axis=0)
            si = pltpu.roll(xi, d, axis=0)
            xr, xi = cmul_add(xr, xi, tab[:, :half], tab[:, half:], sr, si)
        xr, xi = cmul_add(xr, xi, ap[:, :half], ap[:, half:], cr, ci)
        s_scr[pl.ds(r0, SUBLANES), :] = jnp.concatenate([xr, xi], axis=1)
        return xr[SUBLANES - 1:, :], xi[SUBLANES - 1:, :]

    zero = jnp.zeros((1, half), F32)
    lax.fori_loop(0, n_chunks // SUBLANES, block_scan, (zero, zero))

    states = s_scr[...].astype(BF16)
    for n in range(cw // MXU_COLS):
        tile = slice(n * MXU_COLS, (n + 1) * MXU_COLS)
        kdim = _toeplitz_rows(n * per_tile)
        y = jnp.dot(ucv_scr[:, 0:kdim], w1_scr[0:kdim, tile], preferred_element_type=F32)
        y = y + lax.dot_general(states, qct_scr[tile, :], (((1,), (1,)), ((), ())),
                                preferred_element_type=F32)
        y = jax.nn.gelu(y)
        for tl in range(per_tile):
            t = n * per_tile + tl
            y_ref[pl.ds(t, n_chunks, stride=SSM_CHUNK), :] = y[:, tl * LANES:(tl + 1) * LANES]


def _ssm(u, pw_t, bc, drow, atab, *, batch):
    n_tok, d_ssm = u.shape
    seq = n_tok // batch
    cps = seq // SSM_CHUNK
    cw = SSM_CHUNK * LANES
    return pl.pallas_call(
        _ssm_kernel,
        grid=(d_ssm // LANES, batch),
        in_specs=[
            pl.BlockSpec((seq, LANES), lambda b, s: (s, b)),
            pl.BlockSpec((None,) + pw_t.shape[1:], lambda b, s: (b, 0, 0, 0)),
            pl.BlockSpec((None,) + bc.shape[1:], lambda b, s: (b, 0, 0, 0)),
            pl.BlockSpec((None,) + drow.shape[1:], lambda b, s: (b, 0, 0)),
            pl.BlockSpec((None,) + atab.shape[1:], lambda b, s: (b, 0, 0, 0)),
        ],
        out_specs=pl.BlockSpec((seq, LANES), lambda b, s: (s, b)),
        out_shape=jax.ShapeDtypeStruct(u.shape, F32),
        scratch_shapes=[
            pltpu.VMEM((cw, cw + STATE_W), BF16),
            pltpu.VMEM((cw, STATE_W), BF16),
            pltpu.VMEM((cps, STATE_W), F32),
            pltpu.VMEM((cps, STATE_W), F32),
            pltpu.VMEM((cps, cw), BF16),
        ],
        compiler_params=pltpu.CompilerParams(
            dimension_semantics=("arbitrary", "arbitrary"), vmem_limit_bytes=VMEM_LIMIT),
        name="ssm",
    )(u, pw_t, bc, drow, atab)


def _mix_out_kernel(y_ref, c_ref, x_ref, wglu_ref, wout_ref, pg_ref, fg_ref, o_ref, h_ref):
    tm, d_ssm = y_ref.shape
    pr = tm // MIX_PARTS
    for part in range(MIX_PARTS):
        rows = slice(part * pr, (part + 1) * pr)
        y = y_ref[rows, :]
        gate = jnp.dot(y.astype(BF16), wglu_ref[...], preferred_element_type=F32)
        a = y * jax.nn.sigmoid(gate)

        yo = (jnp.dot(a.astype(BF16), wout_ref[0:d_ssm, :], preferred_element_type=F32)
              + jnp.dot(c_ref[rows, :], wout_ref[d_ssm:, :], preferred_element_type=F32))
        x1 = x_ref[rows, :] + _rms(yo, pg_ref[...])
        o_ref[rows, :] = x1
        h_ref[rows, :] = _rms(x1, fg_ref[...]).astype(BF16)


def _mix_out(y, c, x2, w_glu, w_out, post_g, pre_ffn_g, *, tm):
    n_tok, d_model = x2.shape
    d_ssm = y.shape[1]
    d_conv = c.shape[1]
    const = lambda i: (0, 0)
    return pl.pallas_call(
        _mix_out_kernel,
        grid=(n_tok // tm,),
        in_specs=[
            pl.BlockSpec((tm, d_ssm), lambda i: (i, 0)),
            pl.BlockSpec((tm, d_conv), lambda i: (i, 0)),
            pl.BlockSpec((tm, d_model), lambda i: (i, 0)),
            pl.BlockSpec(w_glu.shape, const),
            pl.BlockSpec(w_out.shape, const),
            pl.BlockSpec(post_g.shape, const),
            pl.BlockSpec(pre_ffn_g.shape, const),
        ],
        out_specs=[
            pl.BlockSpec((tm, d_model), lambda i: (i, 0)),
            pl.BlockSpec((tm, d_model), lambda i: (i, 0)),
        ],
        out_shape=[
            jax.ShapeDtypeStruct((n_tok, d_model), F32),
            jax.ShapeDtypeStruct((n_tok, d_model), BF16),
        ],
        compiler_params=pltpu.CompilerParams(
            dimension_semantics=("arbitrary",), vmem_limit_bytes=VMEM_LIMIT),
        name="mix_out",
    )(y, c, x2, w_glu, w_out, post_g, pre_ffn_g)


def _ffn_kernel(x_ref, h_ref, wg_ref, wv_ref, cg_ref, cv_ref, wd_ref, pg_ref, o_ref,
                acc_scr, ext_scr, carry_scr, *, tiles_per_seq):
    i = pl.program_id(0)
    j = pl.program_id(1)
    tm = x_ref.shape[0]

    @pl.when(j == 0)
    def _start_tile():
        acc_scr[...] = jnp.zeros_like(acc_scr)

    seq_start = (i % tiles_per_seq) == 0
    h = h_ref[...]

    def up_pass(w_ref, slot, cols):
        up = jnp.dot(h, w_ref[:, cols], preferred_element_type=F32)
        ext_scr[slot, 0:SUBLANES, cols] = jnp.where(seq_start, 0.0, carry_scr[j, slot, :, cols])
        ext_scr[slot, SUBLANES:, cols] = up
        carry_scr[j, slot, :, cols] = up[tm - SUBLANES:, :]

    def conv(cw_ref, slot, cols):
        out = None
        for k in range(FFN_CONV_K):
            shift = FFN_CONV_K - 1 - k
            term = cw_ref[k:k + 1, cols] * ext_scr[slot, pl.ds(SUBLANES - shift, tm), cols]
            out = term if out is None else out + term
        return out

    passes = [slice(c0, c0 + FFN_COLS) for c0 in range(0, wd_ref.shape[0], FFN_COLS)]
    for cols in passes:
        up_pass(wg_ref, 0, cols)
    for cols in passes:
        up_pass(wv_ref, 1, cols)
    for cols in passes:
        hidden = jax.nn.gelu(conv(cg_ref, 0, cols)) * conv(cv_ref, 1, cols)
        acc_scr[...] += jnp.dot(hidden.astype(BF16), wd_ref[cols, :],
                                preferred_element_type=F32)

    @pl.when(j == pl.num_programs(1) - 1)
    def _finish_tile():
        o_ref[...] = x_ref[...] + _rms(acc_scr[...], pg_ref[...])


def _ffn(x1, h2, w_up, conv_w, w_down, post_g, *, tm, tf, seq):
    n_tok, d_model = x1.shape
    d_ff = w_down.shape[0]
    n_ff = d_ff // tf
    const = lambda i, j: (0, 0)
    return pl.pallas_call(
        functools.partial(_ffn_kernel, tiles_per_seq=seq // tm),
        grid=(n_tok // tm, n_ff),
        in_specs=[
            pl.BlockSpec((tm, d_model), lambda i, j: (i, 0)),
            pl.BlockSpec((tm, d_model), lambda i, j: (i, 0)),
            pl.BlockSpec((d_model, tf), lambda i, j: (0, j)),
            pl.BlockSpec((d_model, tf), lambda i, j: (0, j + n_ff)),
            pl.BlockSpec((FFN_CONV_K, tf), lambda i, j: (0, j)),
            pl.BlockSpec((FFN_CONV_K, tf), lambda i, j: (0, j + n_ff)),
            pl.BlockSpec((tf, d_model), lambda i, j: (j, 0)),
            pl.BlockSpec(post_g.shape, const),
        ],
        out_specs=pl.BlockSpec((tm, d_model), lambda i, j: (i, 0)),
        out_shape=jax.ShapeDtypeStruct((n_tok, d_model), F32),
        scratch_shapes=[
            pltpu.VMEM((tm, d_model), F32),
            pltpu.VMEM((2, tm + SUBLANES, tf), F32),
            pltpu.VMEM((n_ff, 2, SUBLANES, tf), F32),
        ],
        compiler_params=pltpu.CompilerParams(
            dimension_semantics=("arbitrary", "arbitrary"), vmem_limit_bytes=VMEM_LIMIT),
        name="ffn",
    )(x1, h2, w_up, w_up, conv_w, conv_w, w_down, post_g)


def _layer(x, pre_mix_g, w_in, log_dt, lam_re, lam_im, b_re, b_im, c_re, c_im, d, w_glu,
           conv_w, ln_g, ln_b, w_out, post_mix_g, pre_ffn_g, w_up, ffn_conv_w, w_down,
           post_ffn_g, *, tm, tf):
    batch, seq, d_model = x.shape
    d_ssm = w_glu.shape[0]
    d_conv = conv_w.shape[1]
    assert seq % tm == 0 and tm % (SSM_CHUNK * SUBLANES) == 0 and tm % CONV_HALO == 0
    assert d_ssm % LANES == 0 and (seq // SSM_CHUNK) % SUBLANES == 0
    assert d_conv % CONV_COLS == 0 and tf % FFN_COLS == 0
    assert d_ssm % (LANES * (d_conv // CONV_COLS)) == 0
    row = lambda v: v.reshape(1, -1).astype(F32)

    x2 = x.reshape(batch * seq, d_model)
    u, c, w_glu16, w_out16, w_up16, w_down16 = _in_proj(
        x2, row(pre_mix_g), w_in.astype(BF16), conv_w.astype(F32), row(ln_g), row(ln_b),
        [w.astype(F32) for w in (w_glu, w_out, w_up, w_down)],
        d_ssm=d_ssm, d_conv=d_conv, tm=tm, seq=seq)
    ssm_params = _ssm_params(log_dt, lam_re, lam_im, b_re, b_im, c_re, c_im, d)
    y = _ssm(u, *ssm_params, batch=batch)
    x1, h2 = _mix_out(y, c, x2, w_glu16, w_out16, row(post_mix_g), row(pre_ffn_g), tm=tm)
    out = _ffn(x1, h2, w_up16, ffn_conv_w.astype(F32), w_down16, row(post_ffn_g),
               tm=tm, tf=tf, seq=seq)
    return out.reshape(batch, seq, d_model)


def kernel(x, pre_mix_g, w_in, ssm_log_dt, ssm_lam_re, ssm_lam_im, ssm_b_re, ssm_b_im,
           ssm_c_re, ssm_c_im, ssm_d, ssm_w_glu, conv_w, conv_ln_g, conv_ln_b, w_out,
           post_mix_g, pre_ffn_g, ffn_w_up, ffn_conv_w, ffn_w_down, post_ffn_g):
    for i in range(pre_mix_g.shape[0]):
        x = _layer(x, pre_mix_g[i], w_in[i], ssm_log_dt[i], ssm_lam_re[i], ssm_lam_im[i],
                   ssm_b_re[i], ssm_b_im[i], ssm_c_re[i], ssm_c_im[i], ssm_d[i], ssm_w_glu[i],
                   conv_w[i], conv_ln_g[i], conv_ln_b[i], w_out[i], post_mix_g[i],
                   pre_ffn_g[i], ffn_w_up[i], ffn_conv_w[i], ffn_w_down[i], post_ffn_g[i],
                   tm=512, tf=512)
    return x
```

```python
import functools

import jax
import jax.numpy as jnp
from jax import lax
from jax.experimental import pallas as pl
from jax.experimental.pallas import tpu as pltpu

F32 = jnp.float32
BF16 = jnp.bfloat16

EPS = 1e-6
SSM_GROUP = 16
SSM_STATE = 64
CONV_K = 31
FFN_CONV_K = 3

LANES = 128
SUBLANES = 8
BF16_SUBLANES = 16
SSM_CHUNK = 16
GROUPS_PER_BLOCK = LANES // SSM_GROUP
STATE_W = 2 * GROUPS_PER_BLOCK * SSM_STATE
CONV_HALO = 32
MXU_COLS = 256
CONV_COLS = MXU_COLS
FFN_COLS = MXU_COLS
VMEM_LIMIT = 56 * 1024 * 1024
MIX_PARTS = 2


def _rms(x, gain):
    return x * lax.rsqrt(jnp.mean(x * x, axis=-1, keepdims=True) + EPS) * gain


def _in_proj_kernel(x_ref, g_ref, w_ref, cw_ref, lng_ref, lnb_ref, *rest,
                    d_ssm, d_conv, tiles_per_seq, n_cast):
    cast_in, rest = rest[:n_cast], rest[n_cast:]
    (u_ref, c_ref), rest = rest[:2], rest[2:]
    cast_out, rest = rest[:n_cast], rest[n_cast:]
    (zs_scr, co_scr), z_scrs = rest[:2], rest[2:]
    tm = x_ref.shape[0]
    i = pl.program_id(0)

    for src, dst in zip(cast_in, cast_out):
        if len(dst.shape) == 2:
            dst[...] = src[...].astype(BF16)
        else:
            for n in range(dst.shape[0]):
                dst[n] = src[:, n * dst.shape[2]:(n + 1) * dst.shape[2]].astype(BF16)

    @pl.when(i == 0)
    def _init_context():
        for z_scr in z_scrs:
            z_scr[...] = jnp.zeros_like(z_scr)

    seq_start = (i % tiles_per_seq) == 0
    for z_scr in z_scrs:
        z_scr[0:CONV_HALO, :] = jnp.where(seq_start, 0.0, z_scr[tm:tm + CONV_HALO, :])

    h = _rms(x_ref[...], g_ref[...]).astype(BF16)
    shifted_rows = tm + CONV_HALO - SUBLANES
    n_pass = d_conv // CONV_COLS
    u_cols = d_ssm // n_pass

    def glu_pass(cb):
        v0 = d_ssm + cb * CONV_COLS
        g0 = d_ssm + d_conv + cb * CONV_COLS
        cv = jnp.dot(h, w_ref[:, v0:v0 + CONV_COLS], preferred_element_type=F32)
        cg = jnp.dot(h, w_ref[:, g0:g0 + CONV_COLS], preferred_element_type=F32)
        z_scrs[cb][CONV_HALO:, :] = cv * jax.nn.sigmoid(cg)

    def ssm_input_pass(cb):
        cols = slice(cb * u_cols, (cb + 1) * u_cols)
        u_ref[:, cols] = jnp.dot(h, w_ref[:, cols], preferred_element_type=F32)

    def conv_pass(cb):
        cols = slice(cb * CONV_COLS, (cb + 1) * CONV_COLS)
        z_scr = z_scrs[cb]
        for r in range(1, SUBLANES):
            zs_scr[r - 1] = z_scr[pl.ds(r, shifted_rows), :]
        acc = None
        for k in range(CONV_K):
            q, r = divmod(CONV_HALO - (CONV_K - 1) + k, SUBLANES)
            if r == 0:
                src = z_scr[pl.ds(q * SUBLANES, tm), :]
            else:
                src = zs_scr[r - 1, pl.ds(q * SUBLANES, tm), :]
            term = cw_ref[k:k + 1, cols] * src
            acc = term if acc is None else acc + term
        co_scr[:, cols] = acc

    glu_pass(0)
    for cb in range(n_pass):
        if cb + 1 < n_pass:
            glu_pass(cb + 1)
        ssm_input_pass(cb)
        conv_pass(cb)

    co = co_scr[...]
    mu = jnp.mean(co, axis=-1, keepdims=True)
    xc = co - mu
    zn = xc * lax.rsqrt(jnp.mean(xc * xc, axis=-1, keepdims=True) + EPS)
    c_ref[...] = jax.nn.silu(zn * lng_ref[...] + lnb_ref[...]).astype(BF16)


def _in_proj(x2, gain, w_in, conv_w, ln_g, ln_b, to_cast, *, d_ssm, d_conv, tm, seq):
    n_tok, d_model = x2.shape
    steps = n_tok // tm
    const = lambda i: (0, 0)
    mats = [w for w, _ in to_cast]
    for w in mats:
        assert w.shape[0] % (steps * BF16_SUBLANES) == 0
    in_cast_specs = [pl.BlockSpec((w.shape[0] // steps, w.shape[1]), lambda i: (i, 0))
                     for w in mats]
    out_cast_specs, out_cast_shapes = [], []
    for w, col_tile in to_cast:
        rows, cols = w.shape
        if col_tile is None:
            out_cast_specs.append(pl.BlockSpec((rows // steps, cols), lambda i: (i, 0)))
            out_cast_shapes.append(jax.ShapeDtypeStruct((rows, cols), BF16))
        else:
            n_tiles = cols // col_tile
            out_cast_specs.append(
                pl.BlockSpec((n_tiles, rows // steps, col_tile), lambda i: (0, i, 0)))
            out_cast_shapes.append(jax.ShapeDtypeStruct((n_tiles, rows, col_tile), BF16))
    return pl.pallas_call(
        functools.partial(_in_proj_kernel, d_ssm=d_ssm, d_conv=d_conv,
                          tiles_per_seq=seq // tm, n_cast=len(to_cast)),
        grid=(steps,),
        in_specs=[
            pl.BlockSpec((tm, d_model), lambda i: (i, 0)),
            pl.BlockSpec((1, d_model), const),
            pl.BlockSpec(w_in.shape, const, pipeline_mode=pl.Buffered(1)),
            pl.BlockSpec(conv_w.shape, const),
            pl.BlockSpec(ln_g.shape, const),
            pl.BlockSpec(ln_b.shape, const),
        ] + in_cast_specs,
        out_specs=[
            pl.BlockSpec((tm, d_ssm), lambda i: (i, 0)),
            pl.BlockSpec((tm, d_conv), lambda i: (i, 0)),
        ] + out_cast_specs,
        out_shape=[
            jax.ShapeDtypeStruct((n_tok, d_ssm), F32),
            jax.ShapeDtypeStruct((n_tok, d_conv), BF16),
        ] + out_cast_shapes,
        scratch_shapes=[
            pltpu.VMEM((SUBLANES - 1, tm + CONV_HALO - SUBLANES, CONV_COLS), F32),
            pltpu.VMEM((tm, d_conv), F32),
        ] + [pltpu.VMEM((tm + CONV_HALO, CONV_COLS), F32)] * (d_conv // CONV_COLS),
        compiler_params=pltpu.CompilerParams(
            dimension_semantics=("arbitrary",), vmem_limit_bytes=VMEM_LIMIT),
        name="in_proj",
    )(x2, gain, w_in, conv_w, ln_g, ln_b, *mats)


def _ssm_params(log_dt, lam_re, lam_im, b_re, b_im, c_re, c_im, d):
    n_groups = lam_re.shape[0]
    nb = n_groups // GROUPS_PER_BLOCK
    half = STATE_W // 2
    lam = lax.complex(lam_re.astype(F32), lam_im.astype(F32))
    lam_dt = lam * jnp.exp(log_dt.astype(F32))[:, None]
    steps = jnp.arange(SSM_CHUNK + 1, dtype=F32)[:, None, None]
    pw = jnp.exp(lam_dt[None] * steps)
    b_bar = ((pw[1] - 1.0) / lam)[..., None] * lax.complex(b_re.astype(F32), b_im.astype(F32))
    b_t = b_bar.reshape(nb, GROUPS_PER_BLOCK, SSM_STATE, SSM_GROUP).transpose(0, 3, 1, 2)
    b_t = b_t.reshape(nb, SSM_GROUP, half)
    c_t = lax.complex(c_re.astype(F32), c_im.astype(F32))
    c_t = c_t.reshape(nb, GROUPS_PER_BLOCK, SSM_GROUP, SSM_STATE).transpose(0, 2, 1, 3)
    c_t = c_t.reshape(nb, SSM_GROUP, half)
    bc = jnp.stack([b_t.real, b_t.imag, c_t.real, c_t.imag], axis=1)
    pw_t = pw.reshape(SSM_CHUNK + 1, nb, half).transpose(1, 0, 2)
    pw_t = jnp.stack([pw_t.real, pw_t.imag], axis=1)
    drow = d.astype(F32).reshape(nb, 1, LANES)

    row = jnp.arange(SUBLANES)
    expo = jnp.stack([jnp.where(row >= 1, 1, 0), jnp.where(row >= 2, 2, 0),
                      jnp.where(row >= 4, 4, 0), row + 1])
    keep = jnp.stack([row >= 1, row >= 2, row >= 4, row >= 0])
    ap = jnp.exp(lam_dt[None, None] * (SSM_CHUNK * expo.astype(F32))[:, :, None, None])
    ap = jnp.where(keep[:, :, None, None], ap, 0.0)
    ap = jnp.stack([ap.real, ap.imag], axis=2)
    ap = ap.reshape(4, SUBLANES, 2, nb, half)
    atab = ap.transpose(3, 0, 1, 2, 4).reshape(nb, 4, SUBLANES, STATE_W)
    return pw_t, bc, drow, atab


def _toeplitz_rows(k):
    return (k // (MXU_COLS // LANES) + 1) * MXU_COLS


def _ssm_kernel(u_ref, pw_ref, bc_ref, d_ref, at_ref, y_ref, w1_scr, qct_scr, b_scr, s_scr,
                ucv_scr):
    cw = SSM_CHUNK * LANES
    half = STATE_W // 2
    n_chunks = u_ref.shape[0] // SSM_CHUNK

    @pl.when(pl.program_id(1) == 0)
    def _build_operators():
        row_g = lax.broadcasted_iota(jnp.int32, (LANES, STATE_W), 0) // SSM_GROUP
        col_g = (lax.broadcasted_iota(jnp.int32, (LANES, STATE_W), 1) % half) // SSM_STATE
        smask = row_g == col_g
        b_r, b_i, c_r, c_i = bc_ref[0], bc_ref[1], bc_ref[2], bc_ref[3]

        def slab(re, im):
            v = jnp.concatenate([jnp.concatenate([re, im], axis=1)] * GROUPS_PER_BLOCK, axis=0)
            return jnp.where(smask, v, 0.0).astype(BF16)

        for m in range(SSM_CHUNK):
            rows = slice(m * LANES, (m + 1) * LANES)
            e = SSM_CHUNK - 1 - m
            p_r, p_i = pw_ref[0, e:e + 1, :], pw_ref[1, e:e + 1, :]
            w1_scr[rows, cw:] = slab(p_r * b_r - p_i * b_i, p_r * b_i + p_i * b_r)
            q_r, q_i = pw_ref[0, m + 1:m + 2, :], pw_ref[1, m + 1:m + 2, :]
            qct_scr[rows, :] = slab(c_r * q_r - c_i * q_i, -(c_r * q_i + c_i * q_r))

        c0 = slab(c_r, -c_i)
        diag = (lax.broadcasted_iota(jnp.int32, (LANES, LANES), 0)
                == lax.broadcasted_iota(jnp.int32, (LANES, LANES), 1))
        blocks = []
        for m in range(SSM_CHUNK):
            e = SSM_CHUNK - 1 - m
            tap = lax.dot_general(w1_scr[e * LANES:(e + 1) * LANES, cw:], c0,
                                  (((1,), (1,)), ((), ())), preferred_element_type=F32)
            if m == 0:
                tap = tap + jnp.where(diag, d_ref[...], 0.0)
            blocks.append(tap.astype(BF16))
        zero_blk = jnp.zeros((LANES, LANES), BF16)
        for k in range(SSM_CHUNK):
            for j in range(_toeplitz_rows(k) // LANES):
                w1_scr[j * LANES:(j + 1) * LANES, k * LANES:(k + 1) * LANES] = (
                    blocks[k - j] if k >= j else zero_blk)

    per_tile = MXU_COLS // LANES
    bc = None
    for n in range(cw // MXU_COLS):
        tile = slice(n * MXU_COLS, (n + 1) * MXU_COLS)
        for tl in range(per_tile):
            t = n * per_tile + tl
            ucv_scr[:, t * LANES:(t + 1) * LANES] = (
                u_ref[pl.ds(t, n_chunks, stride=SSM_CHUNK), :].astype(BF16))
        part = jnp.dot(ucv_scr[:, tile], w1_scr[tile, cw:], preferred_element_type=F32)
        bc = part if bc is None else bc + part
    bc = pltpu.roll(bc, 1, axis=0)
    first_row = lax.broadcasted_iota(jnp.int32, bc.shape, 0) == 0
    b_scr[...] = jnp.where(first_row, 0.0, bc)

    a1, a2, a4, ap = at_ref[0], at_ref[1], at_ref[2], at_ref[3]

    def cmul_add(xr, xi, ar, ai, sr, si):
        return xr + ar * sr - ai * si, xi + ar * si + ai * sr

    def block_scan(i, carry):
        cr, ci = carry
        r0 = pl.multiple_of(i * SUBLANES, SUBLANES)
        x = b_scr[pl.ds(r0, SUBLANES), :]
        xr, xi = x[:, :half], x[:, half:]
        for d, tab in ((1, a1), (2, a2), (4, a4)):
            sr = pltpu.roll(xr, d, axis=0)
            si = pltpu.roll(xi, d, axis=0)
            xr, xi = cmul_add(xr, xi, tab[:, :half], tab[:, half:], sr, si)
        xr, xi = cmul_add(xr, xi, ap[:, :half], ap[:, half:], cr, ci)
        s_scr[pl.ds(r0, SUBLANES), :] = jnp.concatenate([xr, xi], axis=1)
        return xr[SUBLANES - 1:, :], xi[SUBLANES - 1:, :]

    zero = jnp.zeros((1, half), F32)
    lax.fori_loop(0, n_chunks // SUBLANES, block_scan, (zero, zero))

    states = s_scr[...].astype(BF16)
    for n in range(cw // MXU_COLS):
        tile = slice(n * MXU_COLS, (n + 1) * MXU_COLS)
        kdim = _toeplitz_rows(n * per_tile)
        y = jnp.dot(ucv_scr[:, 0:kdim], w1_scr[0:kdim, tile], preferred_element_type=F32)
        y = y + lax.dot_general(states, qct_scr[tile, :], (((1,), (1,)), ((), ())),
                                preferred_element_type=F32)
        y = jax.nn.gelu(y)
        for tl in range(per_tile):
            t = n * per_tile + tl
            y_ref[pl.ds(t, n_chunks, stride=SSM_CHUNK), :] = y[:, tl * LANES:(tl + 1) * LANES]


def _ssm(u, pw_t, bc, drow, atab, *, batch):
    n_tok, d_ssm = u.shape
    seq = n_tok // batch
    cps = seq // SSM_CHUNK
    cw = SSM_CHUNK * LANES
    return pl.pallas_call(
        _ssm_kernel,
        grid=(d_ssm // LANES, batch),
        in_specs=[
            pl.BlockSpec((seq, LANES), lambda b, s: (s, b)),
            pl.BlockSpec((None,) + pw_t.shape[1:], lambda b, s: (b, 0, 0, 0)),
            pl.BlockSpec((None,) + bc.shape[1:], lambda b, s: (b, 0, 0, 0)),
            pl.BlockSpec((None,) + drow.shape[1:], lambda b, s: (b, 0, 0)),
            pl.BlockSpec((None,) + atab.shape[1:], lambda b, s: (b, 0, 0, 0)),
        ],
        out_specs=pl.BlockSpec((seq, LANES), lambda b, s: (s, b)),
        out_shape=jax.ShapeDtypeStruct(u.shape, F32),
        scratch_shapes=[
            pltpu.VMEM((cw, cw + STATE_W), BF16),
            pltpu.VMEM((cw, STATE_W), BF16),
            pltpu.VMEM((cps, STATE_W), F32),
            pltpu.VMEM((cps, STATE_W), F32),
            pltpu.VMEM((cps, cw), BF16),
        ],
        compiler_params=pltpu.CompilerParams(
            dimension_semantics=("arbitrary", "arbitrary"), vmem_limit_bytes=VMEM_LIMIT),
        name="ssm",
    )(u, pw_t, bc, drow, atab)


def _mix_out_kernel(y_ref, c_ref, x_ref, wglu_ref, wout_ref, pg_ref, fg_ref, o_ref, h_ref):
    tm, d_ssm = y_ref.shape
    pr = tm // MIX_PARTS
    for part in range(MIX_PARTS):
        rows = slice(part * pr, (part + 1) * pr)
        y = y_ref[rows, :]
        gate = jnp.dot(y.astype(BF16), wglu_ref[...], preferred_element_type=F32)
        a = y * jax.nn.sigmoid(gate)

        yo = (jnp.dot(a.astype(BF16), wout_ref[0:d_ssm, :], preferred_element_type=F32)
              + jnp.dot(c_ref[rows, :], wout_ref[d_ssm:, :], preferred_element_type=F32))
        x1 = x_ref[rows, :] + _rms(yo, pg_ref[...])
        o_ref[rows, :] = x1
        h_ref[rows, :] = _rms(x1, fg_ref[...]).astype(BF16)


def _mix_out(y, c, x2, w_glu, w_out, post_g, pre_ffn_g, *, tm):
    n_tok, d_model = x2.shape
    d_ssm = y.shape[1]
    d_conv = c.shape[1]
    const = lambda i: (0, 0)
    return pl.pallas_call(
        _mix_out_kernel,
        grid=(n_tok // tm,),
        in_specs=[
            pl.BlockSpec((tm, d_ssm), lambda i: (i, 0)),
            pl.BlockSpec((tm, d_conv), lambda i: (i, 0)),
            pl.BlockSpec((tm, d_model), lambda i: (i, 0)),
            pl.BlockSpec(w_glu.shape, const),
            pl.BlockSpec(w_out.shape, const),
            pl.BlockSpec(post_g.shape, const),
            pl.BlockSpec(pre_ffn_g.shape, const),
        ],
        out_specs=[
            pl.BlockSpec((tm, d_model), lambda i: (i, 0)),
            pl.BlockSpec((tm, d_model), lambda i: (i, 0)),
        ],
        out_shape=[
            jax.ShapeDtypeStruct((n_tok, d_model), F32),
            jax.ShapeDtypeStruct((n_tok, d_model), BF16),
        ],
        compiler_params=pltpu.CompilerParams(
            dimension_semantics=("arbitrary",), vmem_limit_bytes=VMEM_LIMIT),
        name="mix_out",
    )(y, c, x2, w_glu, w_out, post_g, pre_ffn_g)


def _ffn_kernel(x_ref, h_ref, wg_ref, wv_ref, cg_ref, cv_ref, wd_ref, pg_ref, o_ref,
                acc_scr, ext_scr, carry_scr, *, tiles_per_seq):
    i = pl.program_id(0)
    j = pl.program_id(1)
    tm = x_ref.shape[0]

    @pl.when(j == 0)
    def _start_tile():
        acc_scr[...] = jnp.zeros_like(acc_scr)

    seq_start = (i % tiles_per_seq) == 0
    h = h_ref[...]

    def up_pass(w_ref, slot, cols):
        up = jnp.dot(h, w_ref[:, cols], preferred_element_type=F32)
        ext_scr[slot, 0:SUBLANES, cols] = jnp.where(seq_start, 0.0, carry_scr[j, slot, :, cols])
        ext_scr[slot, SUBLANES:, cols] = up
        carry_scr[j, slot, :, cols] = up[tm - SUBLANES:, :]

    def conv(cw_ref, slot, cols):
        out = None
        for k in range(FFN_CONV_K):
            shift = FFN_CONV_K - 1 - k
            term = cw_ref[k:k + 1, cols] * ext_scr[slot, pl.ds(SUBLANES - shift, tm), cols]
            out = term if out is None else out + term
        return out

    passes = [slice(c0, c0 + FFN_COLS) for c0 in range(0, wd_ref.shape[0], FFN_COLS)]
    for cols in passes:
        up_pass(wg_ref, 0, cols)
    for cols in passes:
        up_pass(wv_ref, 1, cols)
    for cols in passes:
        hidden = jax.nn.gelu(conv(cg_ref, 0, cols)) * conv(cv_ref, 1, cols)
        acc_scr[...] += jnp.dot(hidden.astype(BF16), wd_ref[cols, :],
                                preferred_element_type=F32)

    @pl.when(j == pl.num_programs(1) - 1)
    def _finish_tile():
        o_ref[...] = x_ref[...] + _rms(acc_scr[...], pg_ref[...])


def _ffn(x1, h2, w_up, conv_w, w_down, post_g, *, tm, tf, seq):
    n_tok, d_model = x1.shape
    d_ff = w_down.shape[0]
    n_ff = d_ff // tf
    assert w_up.shape == (2 * n_ff, d_model, tf)
    const = lambda i, j: (0, 0)
    return pl.pallas_call(
        functools.partial(_ffn_kernel, tiles_per_seq=seq // tm),
        grid=(n_tok // tm, n_ff),
        in_specs=[
            pl.BlockSpec((tm, d_model), lambda i, j: (i, 0)),
            pl.BlockSpec((tm, d_model), lambda i, j: (i, 0)),
            pl.BlockSpec((None, d_model, tf), lambda i, j: (j, 0, 0)),
            pl.BlockSpec((None, d_model, tf), lambda i, j: (j + n_ff, 0, 0)),
            pl.BlockSpec((FFN_CONV_K, tf), lambda i, j: (0, j)),
            pl.BlockSpec((FFN_CONV_K, tf), lambda i, j: (0, j + n_ff)),
            pl.BlockSpec((tf, d_model), lambda i, j: (j, 0)),
            pl.BlockSpec(post_g.shape, const),
        ],
        out_specs=pl.BlockSpec((tm, d_model), lambda i, j: (i, 0)),
        out_shape=jax.ShapeDtypeStruct((n_tok, d_model), F32),
        scratch_shapes=[
            pltpu.VMEM((tm, d_model), F32),
            pltpu.VMEM((2, tm + SUBLANES, tf), F32),
            pltpu.VMEM((n_ff, 2, SUBLANES, tf), F32),
        ],
        compiler_params=pltpu.CompilerParams(
            dimension_semantics=("arbitrary", "arbitrary"), vmem_limit_bytes=VMEM_LIMIT),
        name="ffn",
    )(x1, h2, w_up, w_up, conv_w, conv_w, w_down, post_g)


def _layer(x, pre_mix_g, w_in, log_dt, lam_re, lam_im, b_re, b_im, c_re, c_im, d, w_glu,
           conv_w, ln_g, ln_b, w_out, post_mix_g, pre_ffn_g, w_up, ffn_conv_w, w_down,
           post_ffn_g, *, tm, tf):
    batch, seq, d_model = x.shape
    d_ssm = w_glu.shape[0]
    d_conv = conv_w.shape[1]
    assert seq % tm == 0 and tm % (SSM_CHUNK * SUBLANES) == 0 and tm % CONV_HALO == 0
    assert d_ssm % LANES == 0 and (seq // SSM_CHUNK) % SUBLANES == 0
    assert d_conv % CONV_COLS == 0 and tf % FFN_COLS == 0
    assert d_ssm % (LANES * (d_conv // CONV_COLS)) == 0
    row = lambda v: v.reshape(1, -1).astype(F32)

    x2 = x.reshape(batch * seq, d_model)
    u, c, w_glu16, w_out16, w_up16, w_down16 = _in_proj(
        x2, row(pre_mix_g), w_in.astype(BF16), conv_w.astype(F32), row(ln_g), row(ln_b),
        [(w_glu.astype(F32), None), (w_out.astype(F32), None), (w_up.astype(F32), tf),
         (w_down.astype(F32), None)],
        d_ssm=d_ssm, d_conv=d_conv, tm=tm, seq=seq)
    ssm_params = _ssm_params(log_dt, lam_re, lam_im, b_re, b_im, c_re, c_im, d)
    y = _ssm(u, *ssm_params, batch=batch)
    x1, h2 = _mix_out(y, c, x2, w_glu16, w_out16, row(post_mix_g), row(pre_ffn_g), tm=tm)
    out = _ffn(x1, h2, w_up16, ffn_conv_w.astype(F32), w_down16, row(post_ffn_g),
               tm=tm, tf=tf, seq=seq)
    return out.reshape(batch, seq, d_model)


def kernel(x, pre_mix_g, w_in, ssm_log_dt, ssm_lam_re, ssm_lam_im, ssm_b_re, ssm_b_im,
           ssm_c_re, ssm_c_im, ssm_d, ssm_w_glu, conv_w, conv_ln_g, conv_ln_b, w_out,
           post_mix_g, pre_ffn_g, ffn_w_up, ffn_conv_w, ffn_w_down, post_ffn_g):
    for i in range(pre_mix_g.shape[0]):
        x = _layer(x, pre_mix_g[i], w_in[i], ssm_log_dt[i], ssm_lam_re[i], ssm_lam_im[i],
                   ssm_b_re[i], ssm_b_im[i], ssm_c_re[i], ssm_c_im[i], ssm_d[i], ssm_w_glu[i],
                   conv_w[i], conv_ln_g[i], conv_ln_b[i], w_out[i], post_mix_g[i],
                   pre_ffn_g[i], ffn_w_up[i], ffn_conv_w[i], ffn_w_down[i], post_ffn_g[i],
                   tm=512, tf=512)
    return x
```

```python
import functools

import jax
import jax.numpy as jnp
from jax import lax
from jax.experimental import pallas as pl
from jax.experimental.pallas import tpu as pltpu

F32 = jnp.float32
BF16 = jnp.bfloat16

EPS = 1e-6
SSM_GROUP = 16
SSM_STATE = 64
CONV_K = 31
FFN_CONV_K = 3

LANES = 128
SUBLANES = 8
BF16_SUBLANES = 16
SSM_CHUNK = 16
GROUPS_PER_BLOCK = LANES // SSM_GROUP
STATE_W = 2 * GROUPS_PER_BLOCK * SSM_STATE
CONV_HALO = 32
MXU_COLS = 256
CONV_COLS = MXU_COLS
FFN_COLS = MXU_COLS
VMEM_LIMIT = 56 * 1024 * 1024
MIX_PARTS = 2


def _rms(x, gain):
    return x * lax.rsqrt(jnp.mean(x * x, axis=-1, keepdims=True) + EPS) * gain


def _in_proj_kernel(x_ref, g_ref, w_ref, cw_ref, lng_ref, lnb_ref, *rest,
                    d_ssm, d_conv, tiles_per_seq, n_cast):
    cast_in, rest = rest[:n_cast], rest[n_cast:]
    (u_ref, c_ref), rest = rest[:2], rest[2:]
    cast_out, rest = rest[:n_cast], rest[n_cast:]
    (zs_scr, co_scr), z_scrs = rest[:2], rest[2:]
    tm = x_ref.shape[0]
    i = pl.program_id(0)

    for src, dst in zip(cast_in, cast_out):
        if len(dst.shape) == 2:
            dst[...] = src[...].astype(BF16)
        else:
            n_tiles, halves, _, width = dst.shape
            for n in range(n_tiles):
                for g in range(halves):
                    c0 = (g * n_tiles + n) * width
                    dst[n, g] = src[:, c0:c0 + width].astype(BF16)

    @pl.when(i == 0)
    def _init_context():
        for z_scr in z_scrs:
            z_scr[...] = jnp.zeros_like(z_scr)

    seq_start = (i % tiles_per_seq) == 0
    for z_scr in z_scrs:
        z_scr[0:CONV_HALO, :] = jnp.where(seq_start, 0.0, z_scr[tm:tm + CONV_HALO, :])

    h = _rms(x_ref[...], g_ref[...]).astype(BF16)
    shifted_rows = tm + CONV_HALO - SUBLANES
    n_pass = d_conv // CONV_COLS
    u_cols = d_ssm // n_pass

    def glu_pass(cb):
        v0 = d_ssm + cb * CONV_COLS
        g0 = d_ssm + d_conv + cb * CONV_COLS
        cv = jnp.dot(h, w_ref[:, v0:v0 + CONV_COLS], preferred_element_type=F32)
        cg = jnp.dot(h, w_ref[:, g0:g0 + CONV_COLS], preferred_element_type=F32)
        z_scrs[cb][CONV_HALO:, :] = cv * jax.nn.sigmoid(cg)

    def ssm_input_pass(cb):
        cols = slice(cb * u_cols, (cb + 1) * u_cols)
        u_ref[:, cols] = jnp.dot(h, w_ref[:, cols], preferred_element_type=F32)

    def conv_pass(cb):
        cols = slice(cb * CONV_COLS, (cb + 1) * CONV_COLS)
        z_scr = z_scrs[cb]
        for r in range(1, SUBLANES):
            zs_scr[r - 1] = z_scr[pl.ds(r, shifted_rows), :]
        acc = None
        for k in range(CONV_K):
            q, r = divmod(CONV_HALO - (CONV_K - 1) + k, SUBLANES)
            if r == 0:
                src = z_scr[pl.ds(q * SUBLANES, tm), :]
            else:
                src = zs_scr[r - 1, pl.ds(q * SUBLANES, tm), :]
            term = cw_ref[k:k + 1, cols] * src
            acc = term if acc is None else acc + term
        co_scr[:, cols] = acc

    glu_pass(0)
    for cb in range(n_pass):
        if cb + 1 < n_pass:
            glu_pass(cb + 1)
        ssm_input_pass(cb)
        conv_pass(cb)

    co = co_scr[...]
    mu = jnp.mean(co, axis=-1, keepdims=True)
    xc = co - mu
    zn = xc * lax.rsqrt(jnp.mean(xc * xc, axis=-1, keepdims=True) + EPS)
    c_ref[...] = jax.nn.silu(zn * lng_ref[...] + lnb_ref[...]).astype(BF16)


def _in_proj(x2, gain, w_in, conv_w, ln_g, ln_b, to_cast, *, d_ssm, d_conv, tm, seq):
    n_tok, d_model = x2.shape
    steps = n_tok // tm
    const = lambda i: (0, 0)
    mats = [w for w, _ in to_cast]
    for w in mats:
        assert w.shape[0] % (steps * BF16_SUBLANES) == 0
    in_cast_specs = [pl.BlockSpec((w.shape[0] // steps, w.shape[1]), lambda i: (i, 0))
                     for w in mats]
    out_cast_specs, out_cast_shapes = [], []
    for w, col_tile in to_cast:
        rows, cols = w.shape
        if col_tile is None:
            out_cast_specs.append(pl.BlockSpec((rows // steps, cols), lambda i: (i, 0)))
            out_cast_shapes.append(jax.ShapeDtypeStruct((rows, cols), BF16))
        else:
            n_tiles = cols // (2 * col_tile)
            out_cast_specs.append(
                pl.BlockSpec((n_tiles, 2, rows // steps, col_tile), lambda i: (0, 0, i, 0)))
            out_cast_shapes.append(jax.ShapeDtypeStruct((n_tiles, 2, rows, col_tile), BF16))
    return pl.pallas_call(
        functools.partial(_in_proj_kernel, d_ssm=d_ssm, d_conv=d_conv,
                          tiles_per_seq=seq // tm, n_cast=len(to_cast)),
        grid=(steps,),
        in_specs=[
            pl.BlockSpec((tm, d_model), lambda i: (i, 0)),
            pl.BlockSpec((1, d_model), const),
            pl.BlockSpec(w_in.shape, const, pipeline_mode=pl.Buffered(1)),
            pl.BlockSpec(conv_w.shape, const),
            pl.BlockSpec(ln_g.shape, const),
            pl.BlockSpec(ln_b.shape, const),
        ] + in_cast_specs,
        out_specs=[
            pl.BlockSpec((tm, d_ssm), lambda i: (i, 0)),
            pl.BlockSpec((tm, d_conv), lambda i: (i, 0)),
        ] + out_cast_specs,
        out_shape=[
            jax.ShapeDtypeStruct((n_tok, d_ssm), F32),
            jax.ShapeDtypeStruct((n_tok, d_conv), BF16),
        ] + out_cast_shapes,
        scratch_shapes=[
            pltpu.VMEM((SUBLANES - 1, tm + CONV_HALO - SUBLANES, CONV_COLS), F32),
            pltpu.VMEM((tm, d_conv), F32),
        ] + [pltpu.VMEM((tm + CONV_HALO, CONV_COLS), F32)] * (d_conv // CONV_COLS),
        compiler_params=pltpu.CompilerParams(
            dimension_semantics=("arbitrary",), vmem_limit_bytes=VMEM_LIMIT),
        name="in_proj",
    )(x2, gain, w_in, conv_w, ln_g, ln_b, *mats)


def _ssm_params(log_dt, lam_re, lam_im, b_re, b_im, c_re, c_im, d):
    n_groups = lam_re.shape[0]
    nb = n_groups // GROUPS_PER_BLOCK
    half = STATE_W // 2
    lam = lax.complex(lam_re.astype(F32), lam_im.astype(F32))
    lam_dt = lam * jnp.exp(log_dt.astype(F32))[:, None]
    steps = jnp.arange(SSM_CHUNK + 1, dtype=F32)[:, None, None]
    pw = jnp.exp(lam_dt[None] * steps)
    b_bar = ((pw[1] - 1.0) / lam)[..., None] * lax.complex(b_re.astype(F32), b_im.astype(F32))
    b_t = b_bar.reshape(nb, GROUPS_PER_BLOCK, SSM_STATE, SSM_GROUP).transpose(0, 3, 1, 2)
    b_t = b_t.reshape(nb, SSM_GROUP, half)
    c_t = lax.complex(c_re.astype(F32), c_im.astype(F32))
    c_t = c_t.reshape(nb, GROUPS_PER_BLOCK, SSM_GROUP, SSM_STATE).transpose(0, 2, 1, 3)
    c_t = c_t.reshape(nb, SSM_GROUP, half)
    bc = jnp.stack([b_t.real, b_t.imag, c_t.real, c_t.imag], axis=1)
    pw_t = pw.reshape(SSM_CHUNK + 1, nb, half).transpose(1, 0, 2)
    pw_t = jnp.stack([pw_t.real, pw_t.imag], axis=1)
    drow = d.astype(F32).reshape(nb, 1, LANES)

    row = jnp.arange(SUBLANES)
    expo = jnp.stack([jnp.where(row >= 1, 1, 0), jnp.where(row >= 2, 2, 0),
                      jnp.where(row >= 4, 4, 0), row + 1])
    keep = jnp.stack([row >= 1, row >= 2, row >= 4, row >= 0])
    ap = jnp.exp(lam_dt[None, None] * (SSM_CHUNK * expo.astype(F32))[:, :, None, None])
    ap = jnp.where(keep[:, :, None, None], ap, 0.0)
    ap = jnp.stack([ap.real, ap.imag], axis=2)
    ap = ap.reshape(4, SUBLANES, 2, nb, half)
    atab = ap.transpose(3, 0, 1, 2, 4).reshape(nb, 4, SUBLANES, STATE_W)
    return pw_t, bc, drow, atab


def _toeplitz_rows(k):
    return (k // (MXU_COLS // LANES) + 1) * MXU_COLS


def _ssm_kernel(u_ref, pw_ref, bc_ref, d_ref, at_ref, y_ref, w1_scr, qct_scr, b_scr, s_scr,
                ucv_scr):
    cw = SSM_CHUNK * LANES
    half = STATE_W // 2
    n_chunks = u_ref.shape[0] // SSM_CHUNK

    @pl.when(pl.program_id(1) == 0)
    def _build_operators():
        row_g = lax.broadcasted_iota(jnp.int32, (LANES, STATE_W), 0) // SSM_GROUP
        col_g = (lax.broadcasted_iota(jnp.int32, (LANES, STATE_W), 1) % half) // SSM_STATE
        smask = row_g == col_g
        b_r, b_i, c_r, c_i = bc_ref[0], bc_ref[1], bc_ref[2], bc_ref[3]

        def slab(re, im):
            v = jnp.concatenate([jnp.concatenate([re, im], axis=1)] * GROUPS_PER_BLOCK, axis=0)
            return jnp.where(smask, v, 0.0).astype(BF16)

        for m in range(SSM_CHUNK):
            rows = slice(m * LANES, (m + 1) * LANES)
            e = SSM_CHUNK - 1 - m
            p_r, p_i = pw_ref[0, e:e + 1, :], pw_ref[1, e:e + 1, :]
            w1_scr[rows, cw:] = slab(p_r * b_r - p_i * b_i, p_r * b_i + p_i * b_r)
            q_r, q_i = pw_ref[0, m + 1:m + 2, :], pw_ref[1, m + 1:m + 2, :]
            qct_scr[rows, :] = slab(c_r * q_r - c_i * q_i, -(c_r * q_i + c_i * q_r))

        c0 = slab(c_r, -c_i)
        diag = (lax.broadcasted_iota(jnp.int32, (LANES, LANES), 0)
                == lax.broadcasted_iota(jnp.int32, (LANES, LANES), 1))
        blocks = []
        for m in range(SSM_CHUNK):
            e = SSM_CHUNK - 1 - m
            tap = lax.dot_general(w1_scr[e * LANES:(e + 1) * LANES, cw:], c0,
                                  (((1,), (1,)), ((), ())), preferred_element_type=F32)
            if m == 0:
                tap = tap + jnp.where(diag, d_ref[...], 0.0)
            blocks.append(tap.astype(BF16))
        zero_blk = jnp.zeros((LANES, LANES), BF16)
        for k in range(SSM_CHUNK):
            for j in range(_toeplitz_rows(k) // LANES):
                w1_scr[j * LANES:(j + 1) * LANES, k * LANES:(k + 1) * LANES] = (
                    blocks[k - j] if k >= j else zero_blk)

    per_tile = MXU_COLS // LANES
    bc = None
    for n in range(cw // MXU_COLS):
        tile = slice(n * MXU_COLS, (n + 1) * MXU_COLS)
        for tl in range(per_tile):
            t = n * per_tile + tl
            ucv_scr[:, t * LANES:(t + 1) * LANES] = (
                u_ref[pl.ds(t, n_chunks, stride=SSM_CHUNK), :].astype(BF16))
        part = jnp.dot(ucv_scr[:, tile], w1_scr[tile, cw:], preferred_element_type=F32)
        bc = part if bc is None else bc + part
    bc = pltpu.roll(bc, 1, axis=0)
    first_row = lax.broadcasted_iota(jnp.int32, bc.shape, 0) == 0
    b_scr[...] = jnp.where(first_row, 0.0, bc)

    a1, a2, a4, ap = at_ref[0], at_ref[1], at_ref[2], at_ref[3]

    def cmul_add(xr, xi, ar, ai, sr, si):
        return xr + ar * sr - ai * si, xi + ar * si + ai * sr

    def block_scan(i, carry):
        cr, ci = carry
        r0 = pl.multiple_of(i * SUBLANES, SUBLANES)
        x = b_scr[pl.ds(r0, SUBLANES), :]
        xr, xi = x[:, :half], x[:, half:]
        for d, tab in ((1, a1), (2, a2), (4, a4)):
            sr = pltpu.roll(xr, d, axis=0)
            si = pltpu.roll(xi, d, axis=0)
            xr, xi = cmul_add(xr, xi, tab[:, :half], tab[:, half:], sr, si)
        xr, xi = cmul_add(xr, xi, ap[:, :half], ap[:, half:], cr, ci)
        s_scr[pl.ds(r0, SUBLANES), :] = jnp.concatenate([xr, xi], axis=1)
        return xr[SUBLANES - 1:, :], xi[SUBLANES - 1:, :]

    zero = jnp.zeros((1, half), F32)
    lax.fori_loop(0, n_chunks // SUBLANES, block_scan, (zero, zero))

    states = s_scr[...].astype(BF16)
    for n in range(cw // MXU_COLS):
        tile = slice(n * MXU_COLS, (n + 1) * MXU_COLS)
        kdim = _toeplitz_rows(n * per_tile)
        y = jnp.dot(ucv_scr[:, 0:kdim], w1_scr[0:kdim, tile], preferred_element_type=F32)
        y = y + lax.dot_general(states, qct_scr[tile, :], (((1,), (1,)), ((), ())),
                                preferred_element_type=F32)
        y = jax.nn.gelu(y)
        for tl in range(per_tile):
            t = n * per_tile + tl
            y_ref[pl.ds(t, n_chunks, stride=SSM_CHUNK), :] = y[:, tl * LANES:(tl + 1) * LANES]


def _ssm(u, pw_t, bc, drow, atab, *, batch):
    n_tok, d_ssm = u.shape
    seq = n_tok // batch
    cps = seq // SSM_CHUNK
    cw = SSM_CHUNK * LANES
    return pl.pallas_call(
        _ssm_kernel,
        grid=(d_ssm // LANES, batch),
        in_specs=[
            pl.BlockSpec((seq, LANES), lambda b, s: (s, b)),
            pl.BlockSpec((None,) + pw_t.shape[1:], lambda b, s: (b, 0, 0, 0)),
            pl.BlockSpec((None,) + bc.shape[1:], lambda b, s: (b, 0, 0, 0)),
            pl.BlockSpec((None,) + drow.shape[1:], lambda b, s: (b, 0, 0)),
            pl.BlockSpec((None,) + atab.shape[1:], lambda b, s: (b, 0, 0, 0)),
        ],
        out_specs=pl.BlockSpec((seq, LANES), lambda b, s: (s, b)),
        out_shape=jax.ShapeDtypeStruct(u.shape, F32),
        scratch_shapes=[
            pltpu.VMEM((cw, cw + STATE_W), BF16),
            pltpu.VMEM((cw, STATE_W), BF16),
            pltpu.VMEM((cps, STATE_W), F32),
            pltpu.VMEM((cps, STATE_W), F32),
            pltpu.VMEM((cps, cw), BF16),
        ],
        compiler_params=pltpu.CompilerParams(
            dimension_semantics=("arbitrary", "arbitrary"), vmem_limit_bytes=VMEM_LIMIT),
        name="ssm",
    )(u, pw_t, bc, drow, atab)


def _mix_out_kernel(y_ref, c_ref, x_ref, wglu_ref, wout_ref, pg_ref, fg_ref, o_ref, h_ref):
    tm, d_ssm = y_ref.shape
    pr = tm // MIX_PARTS
    for part in range(MIX_PARTS):
        rows = slice(part * pr, (part + 1) * pr)
        y = y_ref[rows, :]
        gate = jnp.dot(y.astype(BF16), wglu_ref[...], preferred_element_type=F32)
        a = y * jax.nn.sigmoid(gate)

        yo = (jnp.dot(a.astype(BF16), wout_ref[0:d_ssm, :], preferred_element_type=F32)
              + jnp.dot(c_ref[rows, :], wout_ref[d_ssm:, :], preferred_element_type=F32))
        x1 = x_ref[rows, :] + _rms(yo, pg_ref[...])
        o_ref[rows, :] = x1
        h_ref[rows, :] = _rms(x1, fg_ref[...]).astype(BF16)


def _mix_out(y, c, x2, w_glu, w_out, post_g, pre_ffn_g, *, tm):
    n_tok, d_model = x2.shape
    d_ssm = y.shape[1]
    d_conv = c.shape[1]
    const = lambda i: (0, 0)
    return pl.pallas_call(
        _mix_out_kernel,
        grid=(n_tok // tm,),
        in_specs=[
            pl.BlockSpec((tm, d_ssm), lambda i: (i, 0)),
            pl.BlockSpec((tm, d_conv), lambda i: (i, 0)),
            pl.BlockSpec((tm, d_model), lambda i: (i, 0)),
            pl.BlockSpec(w_glu.shape, const),
            pl.BlockSpec(w_out.shape, const),
            pl.BlockSpec(post_g.shape, const),
            pl.BlockSpec(pre_ffn_g.shape, const),
        ],
        out_specs=[
            pl.BlockSpec((tm, d_model), lambda i: (i, 0)),
            pl.BlockSpec((tm, d_model), lambda i: (i, 0)),
        ],
        out_shape=[
            jax.ShapeDtypeStruct((n_tok, d_model), F32),
            jax.ShapeDtypeStruct((n_tok, d_model), BF16),
        ],
        compiler_params=pltpu.CompilerParams(
            dimension_semantics=("arbitrary",), vmem_limit_bytes=VMEM_LIMIT),
        name="mix_out",
    )(y, c, x2, w_glu, w_out, post_g, pre_ffn_g)


def _ffn_kernel(x_ref, h_ref, wu_ref, cw_ref, wd_ref, pg_ref, o_ref,
                acc_scr, ext_scr, carry_scr, *, tiles_per_seq):
    i = pl.program_id(0)
    j = pl.program_id(1)
    tm = x_ref.shape[0]

    @pl.when((i == 0) & (j == 0))
    def _first_step():
        acc_scr[...] = jnp.zeros_like(acc_scr)

    seq_start = (i % tiles_per_seq) == 0
    h = h_ref[...]

    def up_pass(slot, cols):
        up = jnp.dot(h, wu_ref[slot, :, cols], preferred_element_type=F32)
        ext_scr[slot, 0:SUBLANES, cols] = jnp.where(seq_start, 0.0, carry_scr[j, slot, :, cols])
        ext_scr[slot, SUBLANES:, cols] = up
        carry_scr[j, slot, :, cols] = up[tm - SUBLANES:, :]

    def conv(slot, cols):
        out = None
        for k in range(FFN_CONV_K):
            shift = FFN_CONV_K - 1 - k
            term = cw_ref[j, slot, k:k + 1, cols] * ext_scr[slot, pl.ds(SUBLANES - shift, tm), cols]
            out = term if out is None else out + term
        return out

    passes = [slice(c0, c0 + FFN_COLS) for c0 in range(0, wd_ref.shape[0], FFN_COLS)]
    for cols in passes:
        up_pass(0, cols)
    for cols in passes:
        up_pass(1, cols)
    for cols in passes:
        hidden = jax.nn.gelu(conv(0, cols)) * conv(1, cols)
        acc_scr[...] += jnp.dot(hidden.astype(BF16), wd_ref[cols, :],
                                preferred_element_type=F32)

    @pl.when(j == pl.num_programs(1) - 1)
    def _finish_tile():
        o_ref[...] = x_ref[...] + _rms(acc_scr[...], pg_ref[...])
        acc_scr[...] = jnp.zeros_like(acc_scr)


def _ffn(x1, h2, w_up, conv_w, w_down, post_g, *, tm, tf, seq):
    n_tok, d_model = x1.shape
    d_ff = w_down.shape[0]
    n_ff = d_ff // tf
    assert w_up.shape == (n_ff, 2, d_model, tf)
    const = lambda i, j: (0, 0)
    conv_t = conv_w.reshape(FFN_CONV_K, 2, n_ff, tf).transpose(2, 1, 0, 3)
    return pl.pallas_call(
        functools.partial(_ffn_kernel, tiles_per_seq=seq // tm),
        grid=(n_tok // tm, n_ff),
        in_specs=[
            pl.BlockSpec((tm, d_model), lambda i, j: (i, 0)),
            pl.BlockSpec((tm, d_model), lambda i, j: (i, 0)),
            pl.BlockSpec((None, 2, d_model, tf), lambda i, j: (j, 0, 0, 0)),
            pl.BlockSpec(conv_t.shape, lambda i, j: (0, 0, 0, 0)),
            pl.BlockSpec((tf, d_model), lambda i, j: (j, 0)),
            pl.BlockSpec(post_g.shape, const),
        ],
        out_specs=pl.BlockSpec((tm, d_model), lambda i, j: (i, 0)),
        out_shape=jax.ShapeDtypeStruct((n_tok, d_model), F32),
        scratch_shapes=[
            pltpu.VMEM((tm, d_model), F32),
            pltpu.VMEM((2, tm + SUBLANES, tf), F32),
            pltpu.VMEM((n_ff, 2, SUBLANES, tf), F32),
        ],
        compiler_params=pltpu.CompilerParams(
            dimension_semantics=("arbitrary", "arbitrary"), vmem_limit_bytes=VMEM_LIMIT),
        name="ffn",
    )(x1, h2, w_up, conv_t, w_down, post_g)


def _layer(x, pre_mix_g, w_in, log_dt, lam_re, lam_im, b_re, b_im, c_re, c_im, d, w_glu,
           conv_w, ln_g, ln_b, w_out, post_mix_g, pre_ffn_g, w_up, ffn_conv_w, w_down,
           post_ffn_g, *, tm, tf):
    batch, seq, d_model = x.shape
    d_ssm = w_glu.shape[0]
    d_conv = conv_w.shape[1]
    assert seq % tm == 0 and tm % (SSM_CHUNK * SUBLANES) == 0 and tm % CONV_HALO == 0
    assert d_ssm % LANES == 0 and (seq // SSM_CHUNK) % SUBLANES == 0
    assert d_conv % CONV_COLS == 0 and tf % FFN_COLS == 0
    assert d_ssm % (LANES * (d_conv // CONV_COLS)) == 0
    row = lambda v: v.reshape(1, -1).astype(F32)

    x2 = x.reshape(batch * seq, d_model)
    u, c, w_glu16, w_out16, w_up16, w_down16 = _in_proj(
        x2, row(pre_mix_g), w_in.astype(BF16), conv_w.astype(F32), row(ln_g), row(ln_b),
        [(w_glu.astype(F32), None), (w_out.astype(F32), None), (w_up.astype(F32), tf),
         (w_down.astype(F32), None)],
        d_ssm=d_ssm, d_conv=d_conv, tm=tm, seq=seq)
    ssm_params = _ssm_params(log_dt, lam_re, lam_im, b_re, b_im, c_re, c_im, d)
    y = _ssm(u, *ssm_params, batch=batch)
    x1, h2 = _mix_out(y, c, x2, w_glu16, w_out16, row(post_mix_g), row(pre_ffn_g), tm=tm)
    out = _ffn(x1, h2, w_up16, ffn_conv_w.astype(F32), w_down16, row(post_ffn_g),
               tm=tm, tf=tf, seq=seq)
    return out.reshape(batch, seq, d_model)


def kernel(x, pre_mix_g, w_in, ssm_log_dt, ssm_lam_re, ssm_lam_im, ssm_b_re, ssm_b_im,
           ssm_c_re, ssm_c_im, ssm_d, ssm_w_glu, conv_w, conv_ln_g, conv_ln_b, w_out,
           post_mix_g, pre_ffn_g, ffn_w_up, ffn_conv_w, ffn_w_down, post_ffn_g):
    for i in range(pre_mix_g.shape[0]):
        x = _layer(x, pre_mix_g[i], w_in[i], ssm_log_dt[i], ssm_lam_re[i], ssm_lam_im[i],
                   ssm_b_re[i], ssm_b_im[i], ssm_c_re[i], ssm_c_im[i], ssm_d[i], ssm_w_glu[i],
                   conv_w[i], conv_ln_g[i], conv_ln_b[i], w_out[i], post_mix_g[i],
                   pre_ffn_g[i], ffn_w_up[i], ffn_conv_w[i], ffn_w_down[i], post_ffn_g[i],
                   tm=512, tf=512)
    return x
```

```python
import functools

import jax
import jax.numpy as jnp
from jax import lax
from jax.experimental import pallas as pl
from jax.experimental.pallas import tpu as pltpu

F32 = jnp.float32
BF16 = jnp.bfloat16

EPS = 1e-6
SSM_GROUP = 16
SSM_STATE = 64
CONV_K = 31
FFN_CONV_K = 3

LANES = 128
SUBLANES = 8
BF16_SUBLANES = 16
SSM_CHUNK = 16
GROUPS_PER_BLOCK = LANES // SSM_GROUP
STATE_W = 2 * GROUPS_PER_BLOCK * SSM_STATE
CONV_HALO = 32
MXU_COLS = 256
CONV_COLS = MXU_COLS
FFN_COLS = MXU_COLS
VMEM_LIMIT = 56 * 1024 * 1024
MIX_PARTS = 2


def _rms(x, gain):
    return x * lax.rsqrt(jnp.mean(x * x, axis=-1, keepdims=True) + EPS) * gain


def _in_proj_kernel(x_ref, g_ref, w_ref, cw_ref, lng_ref, lnb_ref, *rest,
                    d_ssm, d_conv, tiles_per_seq, n_cast):
    cast_in, rest = rest[:n_cast], rest[n_cast:]
    (u_ref, c_ref), rest = rest[:2], rest[2:]
    cast_out, rest = rest[:n_cast], rest[n_cast:]
    (zs_scr, co_scr), z_scrs = rest[:2], rest[2:]
    tm = x_ref.shape[0]
    i = pl.program_id(0)

    for src, dst in zip(cast_in, cast_out):
        if len(dst.shape) == 2:
            dst[...] = src[...].astype(BF16)
        else:
            for n in range(dst.shape[0]):
                dst[n] = src[:, n * dst.shape[2]:(n + 1) * dst.shape[2]].astype(BF16)

    @pl.when(i == 0)
    def _init_context():
        for z_scr in z_scrs:
            z_scr[...] = jnp.zeros_like(z_scr)

    seq_start = (i % tiles_per_seq) == 0
    for z_scr in z_scrs:
        z_scr[0:CONV_HALO, :] = jnp.where(seq_start, 0.0, z_scr[tm:tm + CONV_HALO, :])

    h = _rms(x_ref[...], g_ref[...]).astype(BF16)
    shifted_rows = tm + CONV_HALO - SUBLANES
    n_pass = d_conv // CONV_COLS
    u_cols = d_ssm // n_pass

    def glu_pass(cb):
        v0 = d_ssm + cb * CONV_COLS
        g0 = d_ssm + d_conv + cb * CONV_COLS
        cv = jnp.dot(h, w_ref[:, v0:v0 + CONV_COLS], preferred_element_type=F32)
        cg = jnp.dot(h, w_ref[:, g0:g0 + CONV_COLS], preferred_element_type=F32)
        z_scrs[cb][CONV_HALO:, :] = cv * jax.nn.sigmoid(cg)

    def ssm_input_pass(cb):
        cols = slice(cb * u_cols, (cb + 1) * u_cols)
        u_ref[:, cols] = jnp.dot(h, w_ref[:, cols], preferred_element_type=F32)

    def conv_pass(cb):
        cols = slice(cb * CONV_COLS, (cb + 1) * CONV_COLS)
        z_scr = z_scrs[cb]
        for r in range(1, SUBLANES):
            zs_scr[r - 1] = z_scr[pl.ds(r, shifted_rows), :]
        acc = None
        for k in range(CONV_K):
            q, r = divmod(CONV_HALO - (CONV_K - 1) + k, SUBLANES)
            if r == 0:
                src = z_scr[pl.ds(q * SUBLANES, tm), :]
            else:
                src = zs_scr[r - 1, pl.ds(q * SUBLANES, tm), :]
            term = cw_ref[k:k + 1, cols] * src
            acc = term if acc is None else acc + term
        co_scr[:, cols] = acc

    glu_pass(0)
    for cb in range(n_pass):
        if cb + 1 < n_pass:
            glu_pass(cb + 1)
        ssm_input_pass(cb)
        conv_pass(cb)

    co = co_scr[...]
    mu = jnp.mean(co, axis=-1, keepdims=True)
    xc = co - mu
    zn = xc * lax.rsqrt(jnp.mean(xc * xc, axis=-1, keepdims=True) + EPS)
    c_ref[...] = jax.nn.silu(zn * lng_ref[...] + lnb_ref[...]).astype(BF16)


def _in_proj(x2, gain, w_in, conv_w, ln_g, ln_b, to_cast, *, d_ssm, d_conv, tm, seq):
    n_tok, d_model = x2.shape
    steps = n_tok // tm
    const = lambda i: (0, 0)
    mats = [w for w, _ in to_cast]
    for w in mats:
        assert w.shape[0] % (steps * BF16_SUBLANES) == 0
    in_cast_specs = [pl.BlockSpec((w.shape[0] // steps, w.shape[1]), lambda i: (i, 0))
                     for w in mats]
    out_cast_specs, out_cast_shapes = [], []
    for w, col_tile in to_cast:
        rows, cols = w.shape
        if col_tile is None:
            out_cast_specs.append(pl.BlockSpec((rows // steps, cols), lambda i: (i, 0)))
            out_cast_shapes.append(jax.ShapeDtypeStruct((rows, cols), BF16))
        else:
            n_tiles = cols // col_tile
            out_cast_specs.append(
                pl.BlockSpec((n_tiles, rows // steps, col_tile), lambda i: (0, i, 0)))
            out_cast_shapes.append(jax.ShapeDtypeStruct((n_tiles, rows, col_tile), BF16))
    return pl.pallas_call(
        functools.partial(_in_proj_kernel, d_ssm=d_ssm, d_conv=d_conv,
                          tiles_per_seq=seq // tm, n_cast=len(to_cast)),
        grid=(steps,),
        in_specs=[
            pl.BlockSpec((tm, d_model), lambda i: (i, 0)),
            pl.BlockSpec((1, d_model), const),
            pl.BlockSpec(w_in.shape, const, pipeline_mode=pl.Buffered(1)),
            pl.BlockSpec(conv_w.shape, const),
            pl.BlockSpec(ln_g.shape, const),
            pl.BlockSpec(ln_b.shape, const),
        ] + in_cast_specs,
        out_specs=[
            pl.BlockSpec((tm, d_ssm), lambda i: (i, 0)),
            pl.BlockSpec((tm, d_conv), lambda i: (i, 0)),
        ] + out_cast_specs,
        out_shape=[
            jax.ShapeDtypeStruct((n_tok, d_ssm), F32),
            jax.ShapeDtypeStruct((n_tok, d_conv), BF16),
        ] + out_cast_shapes,
        scratch_shapes=[
            pltpu.VMEM((SUBLANES - 1, tm + CONV_HALO - SUBLANES, CONV_COLS), F32),
            pltpu.VMEM((tm, d_conv), F32),
        ] + [pltpu.VMEM((tm + CONV_HALO, CONV_COLS), F32)] * (d_conv // CONV_COLS),
        compiler_params=pltpu.CompilerParams(
            dimension_semantics=("arbitrary",), vmem_limit_bytes=VMEM_LIMIT),
        name="in_proj",
    )(x2, gain, w_in, conv_w, ln_g, ln_b, *mats)


def _ssm_params(log_dt, lam_re, lam_im, b_re, b_im, c_re, c_im, d):
    n_groups = lam_re.shape[0]
    nb = n_groups // GROUPS_PER_BLOCK
    half = STATE_W // 2
    lam = lax.complex(lam_re.astype(F32), lam_im.astype(F32))
    lam_dt = lam * jnp.exp(log_dt.astype(F32))[:, None]
    steps = jnp.arange(SSM_CHUNK + 1, dtype=F32)[:, None, None]
    pw = jnp.exp(lam_dt[None] * steps)
    b_bar = ((pw[1] - 1.0) / lam)[..., None] * lax.complex(b_re.astype(F32), b_im.astype(F32))
    b_t = b_bar.reshape(nb, GROUPS_PER_BLOCK, SSM_STATE, SSM_GROUP).transpose(0, 3, 1, 2)
    b_t = b_t.reshape(nb, SSM_GROUP, half)
    c_t = lax.complex(c_re.astype(F32), c_im.astype(F32))
    c_t = c_t.reshape(nb, GROUPS_PER_BLOCK, SSM_GROUP, SSM_STATE).transpose(0, 2, 1, 3)
    c_t = c_t.reshape(nb, SSM_GROUP, half)
    bc = jnp.stack([b_t.real, b_t.imag, c_t.real, c_t.imag], axis=1)
    pw_t = pw.reshape(SSM_CHUNK + 1, nb, half).transpose(1, 0, 2)
    pw_t = jnp.stack([pw_t.real, pw_t.imag], axis=1)
    drow = d.astype(F32).reshape(nb, 1, LANES)

    row = jnp.arange(SUBLANES)
    expo = jnp.stack([jnp.where(row >= 1, 1, 0), jnp.where(row >= 2, 2, 0),
                      jnp.where(row >= 4, 4, 0), row + 1])
    keep = jnp.stack([row >= 1, row >= 2, row >= 4, row >= 0])
    ap = jnp.exp(lam_dt[None, None] * (SSM_CHUNK * expo.astype(F32))[:, :, None, None])
    ap = jnp.where(keep[:, :, None, None], ap, 0.0)
    ap = jnp.stack([ap.real, ap.imag], axis=2)
    ap = ap.reshape(4, SUBLANES, 2, nb, half)
    atab = ap.transpose(3, 0, 1, 2, 4).reshape(nb, 4, SUBLANES, STATE_W)
    return pw_t, bc, drow, atab


def _toeplitz_rows(k):
    return (k // (MXU_COLS // LANES) + 1) * MXU_COLS


def _ssm_kernel(u_ref, pw_ref, bc_ref, d_ref, at_ref, y_ref, w1_scr, qct_scr, b_scr, s_scr,
                ucv_scr):
    cw = SSM_CHUNK * LANES
    half = STATE_W // 2
    n_chunks = u_ref.shape[0] // SSM_CHUNK

    @pl.when(pl.program_id(1) == 0)
    def _build_operators():
        row_g = lax.broadcasted_iota(jnp.int32, (LANES, STATE_W), 0) // SSM_GROUP
        col_g = (lax.broadcasted_iota(jnp.int32, (LANES, STATE_W), 1) % half) // SSM_STATE
        smask = row_g == col_g
        b_r, b_i, c_r, c_i = bc_ref[0], bc_ref[1], bc_ref[2], bc_ref[3]

        def slab(re, im):
            v = jnp.concatenate([jnp.concatenate([re, im], axis=1)] * GROUPS_PER_BLOCK, axis=0)
            return jnp.where(smask, v, 0.0).astype(BF16)

        for m in range(SSM_CHUNK):
            rows = slice(m * LANES, (m + 1) * LANES)
            e = SSM_CHUNK - 1 - m
            p_r, p_i = pw_ref[0, e:e + 1, :], pw_ref[1, e:e + 1, :]
            w1_scr[rows, cw:] = slab(p_r * b_r - p_i * b_i, p_r * b_i + p_i * b_r)
            q_r, q_i = pw_ref[0, m + 1:m + 2, :], pw_ref[1, m + 1:m + 2, :]
            qct_scr[rows, :] = slab(c_r * q_r - c_i * q_i, -(c_r * q_i + c_i * q_r))

        c0 = slab(c_r, -c_i)
        diag = (lax.broadcasted_iota(jnp.int32, (LANES, LANES), 0)
                == lax.broadcasted_iota(jnp.int32, (LANES, LANES), 1))
        blocks = []
        for m in range(SSM_CHUNK):
            e = SSM_CHUNK - 1 - m
            tap = lax.dot_general(w1_scr[e * LANES:(e + 1) * LANES, cw:], c0,
                                  (((1,), (1,)), ((), ())), preferred_element_type=F32)
            if m == 0:
                tap = tap + jnp.where(diag, d_ref[...], 0.0)
            blocks.append(tap.astype(BF16))
        zero_blk = jnp.zeros((LANES, LANES), BF16)
        for k in range(SSM_CHUNK):
            for j in range(_toeplitz_rows(k) // LANES):
                w1_scr[j * LANES:(j + 1) * LANES, k * LANES:(k + 1) * LANES] = (
                    blocks[k - j] if k >= j else zero_blk)

    per_tile = MXU_COLS // LANES
    bc = None
    for n in range(cw // MXU_COLS):
        tile = slice(n * MXU_COLS, (n + 1) * MXU_COLS)
        for tl in range(per_tile):
            t = n * per_tile + tl
            ucv_scr[:, t * LANES:(t + 1) * LANES] = (
                u_ref[pl.ds(t, n_chunks, stride=SSM_CHUNK), :].astype(BF16))
        part = jnp.dot(ucv_scr[:, tile], w1_scr[tile, cw:], preferred_element_type=F32)
        bc = part if bc is None else bc + part
    bc = pltpu.roll(bc, 1, axis=0)
    first_row = lax.broadcasted_iota(jnp.int32, bc.shape, 0) == 0
    b_scr[...] = jnp.where(first_row, 0.0, bc)

    a1, a2, a4, ap = at_ref[0], at_ref[1], at_ref[2], at_ref[3]

    def cmul_add(xr, xi, ar, ai, sr, si):
        return xr + ar * sr - ai * si, xi + ar * si + ai * sr

    def block_scan(i, carry):
        cr, ci = carry
        r0 = pl.multiple_of(i * SUBLANES, SUBLANES)
        x = b_scr[pl.ds(r0, SUBLANES), :]
        xr, xi = x[:, :half], x[:, half:]
        for d, tab in ((1, a1), (2, a2), (4, a4)):
            sr = pltpu.roll(xr, d, axis=0)
            si = pltpu.roll(xi, d, axis=0)
            xr, xi = cmul_add(xr, xi, tab[:, :half], tab[:, half:], sr, si)
        xr, xi = cmul_add(xr, xi, ap[:, :half], ap[:, half:], cr, ci)
        s_scr[pl.ds(r0, SUBLANES), :] = jnp.concatenate([xr, xi], axis=1)
        return xr[SUBLANES - 1:, :], xi[SUBLANES - 1:, :]

    zero = jnp.zeros((1, half), F32)
    lax.fori_loop(0, n_chunks // SUBLANES, block_scan, (zero, zero))

    states = s_scr[...].astype(BF16)
    for n in range(cw // MXU_COLS):
        tile = slice(n * MXU_COLS, (n + 1) * MXU_COLS)
        kdim = _toeplitz_rows(n * per_tile)
        y = jnp.dot(ucv_scr[:, 0:kdim], w1_scr[0:kdim, tile], preferred_element_type=F32)
        y = y + lax.dot_general(states, qct_scr[tile, :], (((1,), (1,)), ((), ())),
                                preferred_element_type=F32)
        y = jax.nn.gelu(y)
        for tl in range(per_tile):
            t = n * per_tile + tl
            y_ref[pl.ds(t, n_chunks, stride=SSM_CHUNK), :] = y[:, tl * LANES:(tl + 1) * LANES]


def _ssm(u, pw_t, bc, drow, atab, *, batch):
    n_tok, d_ssm = u.shape
    seq = n_tok // batch
    cps = seq // SSM_CHUNK
    cw = SSM_CHUNK * LANES
    return pl.pallas_call(
        _ssm_kernel,
        grid=(d_ssm // LANES, batch),
        in_specs=[
            pl.BlockSpec((seq, LANES), lambda b, s: (s, b)),
            pl.BlockSpec((None,) + pw_t.shape[1:], lambda b, s: (b, 0, 0, 0)),
            pl.BlockSpec((None,) + bc.shape[1:], lambda b, s: (b, 0, 0, 0)),
            pl.BlockSpec((None,) + drow.shape[1:], lambda b, s: (b, 0, 0)),
            pl.BlockSpec((None,) + atab.shape[1:], lambda b, s: (b, 0, 0, 0)),
        ],
        out_specs=pl.BlockSpec((seq, LANES), lambda b, s: (s, b)),
        out_shape=jax.ShapeDtypeStruct(u.shape, F32),
        scratch_shapes=[
            pltpu.VMEM((cw, cw + STATE_W), BF16),
            pltpu.VMEM((cw, STATE_W), BF16),
            pltpu.VMEM((cps, STATE_W), F32),
            pltpu.VMEM((cps, STATE_W), F32),
            pltpu.VMEM((cps, cw), BF16),
        ],
        compiler_params=pltpu.CompilerParams(
            dimension_semantics=("arbitrary", "arbitrary"), vmem_limit_bytes=VMEM_LIMIT),
        name="ssm",
    )(u, pw_t, bc, drow, atab)


def _mix_out_kernel(y_ref, c_ref, x_ref, wglu_ref, wout_ref, pg_ref, fg_ref, o_ref, h_ref):
    tm, d_ssm = y_ref.shape
    pr = tm // MIX_PARTS
    for part in range(MIX_PARTS):
        rows = slice(part * pr, (part + 1) * pr)
        y = y_ref[rows, :]
        gate = jnp.dot(y.astype(BF16), wglu_ref[...], preferred_element_type=F32)
        a = y * jax.nn.sigmoid(gate)

        yo = (jnp.dot(a.astype(BF16), wout_ref[0:d_ssm, :], preferred_element_type=F32)
              + jnp.dot(c_ref[rows, :], wout_ref[d_ssm:, :], preferred_element_type=F32))
        x1 = x_ref[rows, :] + _rms(yo, pg_ref[...])
        o_ref[rows, :] = x1
        h_ref[rows, :] = _rms(x1, fg_ref[...]).astype(BF16)


def _mix_out(y, c, x2, w_glu, w_out, post_g, pre_ffn_g, *, tm):
    n_tok, d_model = x2.shape
    d_ssm = y.shape[1]
    d_conv = c.shape[1]
    const = lambda i: (0, 0)
    return pl.pallas_call(
        _mix_out_kernel,
        grid=(n_tok // tm,),
        in_specs=[
            pl.BlockSpec((tm, d_ssm), lambda i: (i, 0)),
            pl.BlockSpec((tm, d_conv), lambda i: (i, 0)),
            pl.BlockSpec((tm, d_model), lambda i: (i, 0)),
            pl.BlockSpec(w_glu.shape, const),
            pl.BlockSpec(w_out.shape, const),
            pl.BlockSpec(post_g.shape, const),
            pl.BlockSpec(pre_ffn_g.shape, const),
        ],
        out_specs=[
            pl.BlockSpec((tm, d_model), lambda i: (i, 0)),
            pl.BlockSpec((tm, d_model), lambda i: (i, 0)),
        ],
        out_shape=[
            jax.ShapeDtypeStruct((n_tok, d_model), F32),
            jax.ShapeDtypeStruct((n_tok, d_model), BF16),
        ],
        compiler_params=pltpu.CompilerParams(
            dimension_semantics=("arbitrary",), vmem_limit_bytes=VMEM_LIMIT),
        name="mix_out",
    )(y, c, x2, w_glu, w_out, post_g, pre_ffn_g)


def _ffn_kernel(x_ref, h_ref, wg_ref, wv_ref, cg_ref, cv_ref, wd_ref, pg_ref, o_ref,
                acc_scr, ext_scr, carry_scr, *, tiles_per_seq):
    i = pl.program_id(0)
    j = pl.program_id(1)
    tm = x_ref.shape[0]

    @pl.when(j == 0)
    def _start_tile():
        acc_scr[...] = jnp.zeros_like(acc_scr)

    seq_start = (i % tiles_per_seq) == 0
    h = h_ref[...]

    def up_pass(w_ref, slot, cols):
        up = jnp.dot(h, w_ref[:, cols], preferred_element_type=F32)
        ext_scr[slot, 0:SUBLANES, cols] = jnp.where(seq_start, 0.0, carry_scr[j, slot, :, cols])
        ext_scr[slot, SUBLANES:, cols] = up
        carry_scr[j, slot, :, cols] = up[tm - SUBLANES:, :]

    def conv(cw_ref, slot, cols):
        out = None
        for k in range(FFN_CONV_K):
            shift = FFN_CONV_K - 1 - k
            term = cw_ref[k:k + 1, cols] * ext_scr[slot, pl.ds(SUBLANES - shift, tm), cols]
            out = term if out is None else out + term
        return out

    passes = [slice(c0, c0 + FFN_COLS) for c0 in range(0, wd_ref.shape[0], FFN_COLS)]
    for cols in passes:
        up_pass(wg_ref, 0, cols)
    for cols in passes:
        up_pass(wv_ref, 1, cols)
    for cols in passes:
        hidden = jax.nn.gelu(conv(cg_ref, 0, cols)) * conv(cv_ref, 1, cols)
        acc_scr[...] += jnp.dot(hidden.astype(BF16), wd_ref[cols, :],
                                preferred_element_type=F32)

    @pl.when(j == pl.num_programs(1) - 1)
    def _finish_tile():
        o_ref[...] = x_ref[...] + _rms(acc_scr[...], pg_ref[...])


def _ffn(x1, h2, w_up, conv_w, w_down, post_g, *, tm, tf, seq):
    n_tok, d_model = x1.shape
    d_ff = w_down.shape[0]
    n_ff = d_ff // tf
    assert w_up.shape == (2 * n_ff, d_model, tf)
    const = lambda i, j: (0, 0)
    return pl.pallas_call(
        functools.partial(_ffn_kernel, tiles_per_seq=seq // tm),
        grid=(n_tok // tm, n_ff),
        in_specs=[
            pl.BlockSpec((tm, d_model), lambda i, j: (i, 0)),
            pl.BlockSpec((tm, d_model), lambda i, j: (i, 0)),
            pl.BlockSpec((None, d_model, tf), lambda i, j: (j, 0, 0)),
            pl.BlockSpec((None, d_model, tf), lambda i, j: (j + n_ff, 0, 0)),
            pl.BlockSpec((FFN_CONV_K, tf), lambda i, j: (0, j)),
            pl.BlockSpec((FFN_CONV_K, tf), lambda i, j: (0, j + n_ff)),
            pl.BlockSpec((tf, d_model), lambda i, j: (j, 0)),
            pl.BlockSpec(post_g.shape, const),
        ],
        out_specs=pl.BlockSpec((tm, d_model), lambda i, j: (i, 0)),
        out_shape=jax.ShapeDtypeStruct((n_tok, d_model), F32),
        scratch_shapes=[
            pltpu.VMEM((tm, d_model), F32),
            pltpu.VMEM((2, tm + SUBLANES, tf), F32),
            pltpu.VMEM((n_ff, 2, SUBLANES, tf), F32),
        ],
        compiler_params=pltpu.CompilerParams(
            dimension_semantics=("arbitrary", "arbitrary"), vmem_limit_bytes=VMEM_LIMIT),
        name="ffn",
    )(x1, h2, w_up, w_up, conv_w, conv_w, w_down, post_g)


def _layer(x, pre_mix_g, w_in, log_dt, lam_re, lam_im, b_re, b_im, c_re, c_im, d, w_glu,
           conv_w, ln_g, ln_b, w_out, post_mix_g, pre_ffn_g, w_up, ffn_conv_w, w_down,
           post_ffn_g, *, tm, tf):
    batch, seq, d_model = x.shape
    d_ssm = w_glu.shape[0]
    d_conv = conv_w.shape[1]
    assert seq % tm == 0 and tm % CONV_HALO == 0 and tm % (MIX_PARTS * BF16_SUBLANES) == 0
    assert d_ssm % LANES == 0 and (seq // SSM_CHUNK) % SUBLANES == 0
    assert d_conv % CONV_COLS == 0 and tf % FFN_COLS == 0
    assert d_ssm % (LANES * (d_conv // CONV_COLS)) == 0
    row = lambda v: v.reshape(1, -1).astype(F32)

    x2 = x.reshape(batch * seq, d_model)
    u, c, w_glu16, w_out16, w_up16, w_down16 = _in_proj(
        x2, row(pre_mix_g), w_in.astype(BF16), conv_w.astype(F32), row(ln_g), row(ln_b),
        [(w_glu.astype(F32), None), (w_out.astype(F32), None), (w_up.astype(F32), tf),
         (w_down.astype(F32), None)],
        d_ssm=d_ssm, d_conv=d_conv, tm=tm, seq=seq)
    ssm_params = _ssm_params(log_dt, lam_re, lam_im, b_re, b_im, c_re, c_im, d)
    y = _ssm(u, *ssm_params, batch=batch)
    x1, h2 = _mix_out(y, c, x2, w_glu16, w_out16, row(post_mix_g), row(pre_ffn_g), tm=tm)
    out = _ffn(x1, h2, w_up16, ffn_conv_w.astype(F32), w_down16, row(post_ffn_g),
               tm=tm, tf=tf, seq=seq)
    return out.reshape(batch, seq, d_model)


def kernel(x, pre_mix_g, w_in, ssm_log_dt, ssm_lam_re, ssm_lam_im, ssm_b_re, ssm_b_im,
           ssm_c_re, ssm_c_im, ssm_d, ssm_w_glu, conv_w, conv_ln_g, conv_ln_b, w_out,
           post_mix_g, pre_ffn_g, ffn_w_up, ffn_conv_w, ffn_w_down, post_ffn_g):
    for i in range(pre_mix_g.shape[0]):
        x = _layer(x, pre_mix_g[i], w_in[i], ssm_log_dt[i], ssm_lam_re[i], ssm_lam_im[i],
                   ssm_b_re[i], ssm_b_im[i], ssm_c_re[i], ssm_c_im[i], ssm_d[i], ssm_w_glu[i],
                   conv_w[i], conv_ln_g[i], conv_ln_b[i], w_out[i], post_mix_g[i],
                   pre_ffn_g[i], ffn_w_up[i], ffn_conv_w[i], ffn_w_down[i], post_ffn_g[i],
                   tm=512, tf=512)
    return x
```

```python
import functools

import jax
import jax.numpy as jnp
from jax import lax
from jax.experimental import pallas as pl
from jax.experimental.pallas import tpu as pltpu

F32 = jnp.float32
BF16 = jnp.bfloat16

EPS = 1e-6
SSM_GROUP = 16
SSM_STATE = 64
CONV_K = 31
FFN_CONV_K = 3

LANES = 128
SUBLANES = 8
BF16_SUBLANES = 16
SSM_CHUNK = 16
GROUPS_PER_BLOCK = LANES // SSM_GROUP
STATE_W = 2 * GROUPS_PER_BLOCK * SSM_STATE
CONV_HALO = 32
MXU_COLS = 256
CONV_COLS = MXU_COLS
FFN_COLS = MXU_COLS
VMEM_LIMIT = 56 * 1024 * 1024
MIX_PARTS = 2


def _rms(x, gain):
    return x * lax.rsqrt(jnp.mean(x * x, axis=-1, keepdims=True) + EPS) * gain


def _in_proj_kernel(x_ref, g_ref, w_ref, cw_ref, lng_ref, lnb_ref, *rest,
                    d_ssm, d_conv, tiles_per_seq, n_cast):
    cast_in, rest = rest[:n_cast], rest[n_cast:]
    (u_ref, c_ref), rest = rest[:2], rest[2:]
    cast_out, rest = rest[:n_cast], rest[n_cast:]
    (zs_scr, co_scr), z_scrs = rest[:2], rest[2:]
    tm = x_ref.shape[0]
    i = pl.program_id(0)

    for src, dst in zip(cast_in, cast_out):
        if len(dst.shape) == 2:
            dst[...] = src[...].astype(BF16)
        else:
            for n in range(dst.shape[0]):
                dst[n] = src[:, n * dst.shape[2]:(n + 1) * dst.shape[2]].astype(BF16)

    @pl.when(i == 0)
    def _init_context():
        for z_scr in z_scrs:
            z_scr[...] = jnp.zeros_like(z_scr)

    seq_start = (i % tiles_per_seq) == 0
    for z_scr in z_scrs:
        z_scr[0:CONV_HALO, :] = jnp.where(seq_start, 0.0, z_scr[tm:tm + CONV_HALO, :])

    h = _rms(x_ref[...], g_ref[...]).astype(BF16)
    shifted_rows = tm + CONV_HALO - SUBLANES
    n_pass = d_conv // CONV_COLS
    u_cols = d_ssm // n_pass

    def glu_pass(cb):
        v0 = d_ssm + cb * CONV_COLS
        g0 = d_ssm + d_conv + cb * CONV_COLS
        cv = jnp.dot(h, w_ref[:, v0:v0 + CONV_COLS], preferred_element_type=F32)
        cg = jnp.dot(h, w_ref[:, g0:g0 + CONV_COLS], preferred_element_type=F32)
        z_scrs[cb][CONV_HALO:, :] = cv * jax.nn.sigmoid(cg)

    def ssm_input_pass(cb):
        cols = slice(cb * u_cols, (cb + 1) * u_cols)
        u_ref[:, cols] = jnp.dot(h, w_ref[:, cols], preferred_element_type=F32)

    def conv_pass(cb):
        cols = slice(cb * CONV_COLS, (cb + 1) * CONV_COLS)
        z_scr = z_scrs[cb]
        for r in range(1, SUBLANES):
            zs_scr[r - 1] = z_scr[pl.ds(r, shifted_rows), :]
        acc = None
        for k in range(CONV_K):
            q, r = divmod(CONV_HALO - (CONV_K - 1) + k, SUBLANES)
            if r == 0:
                src = z_scr[pl.ds(q * SUBLANES, tm), :]
            else:
                src = zs_scr[r - 1, pl.ds(q * SUBLANES, tm), :]
            term = cw_ref[k:k + 1, cols] * src
            acc = term if acc is None else acc + term
        co_scr[:, cols] = acc

    glu_pass(0)
    for cb in range(n_pass):
        if cb + 1 < n_pass:
            glu_pass(cb + 1)
        ssm_input_pass(cb)
        conv_pass(cb)

    co = co_scr[...]
    mu = jnp.mean(co, axis=-1, keepdims=True)
    xc = co - mu
    zn = xc * lax.rsqrt(jnp.mean(xc * xc, axis=-1, keepdims=True) + EPS)
    c_ref[...] = jax.nn.silu(zn * lng_ref[...] + lnb_ref[...]).astype(BF16)


def _in_proj(x2, gain, w_in, conv_w, ln_g, ln_b, to_cast, *, d_ssm, d_conv, tm, seq):
    n_tok, d_model = x2.shape
    steps = n_tok // tm
    const = lambda i: (0, 0)
    mats = [w for w, _ in to_cast]
    for w in mats:
        assert w.shape[0] % (steps * BF16_SUBLANES) == 0
    in_cast_specs = [pl.BlockSpec((w.shape[0] // steps, w.shape[1]), lambda i: (i, 0))
                     for w in mats]
    out_cast_specs, out_cast_shapes = [], []
    for w, col_tile in to_cast:
        rows, cols = w.shape
        if col_tile is None:
            out_cast_specs.append(pl.BlockSpec((rows // steps, cols), lambda i: (i, 0)))
            out_cast_shapes.append(jax.ShapeDtypeStruct((rows, cols), BF16))
        else:
            n_tiles = cols // col_tile
            out_cast_specs.append(
                pl.BlockSpec((n_tiles, rows // steps, col_tile), lambda i: (0, i, 0)))
            out_cast_shapes.append(jax.ShapeDtypeStruct((n_tiles, rows, col_tile), BF16))
    return pl.pallas_call(
        functools.partial(_in_proj_kernel, d_ssm=d_ssm, d_conv=d_conv,
                          tiles_per_seq=seq // tm, n_cast=len(to_cast)),
        grid=(steps,),
        in_specs=[
            pl.BlockSpec((tm, d_model), lambda i: (i, 0)),
            pl.BlockSpec((1, d_model), const),
            pl.BlockSpec(w_in.shape, const, pipeline_mode=pl.Buffered(1)),
            pl.BlockSpec(conv_w.shape, const),
            pl.BlockSpec(ln_g.shape, const),
            pl.BlockSpec(ln_b.shape, const),
        ] + in_cast_specs,
        out_specs=[
            pl.BlockSpec((tm, d_ssm), lambda i: (i, 0)),
            pl.BlockSpec((tm, d_conv), lambda i: (i, 0)),
        ] + out_cast_specs,
        out_shape=[
            jax.ShapeDtypeStruct((n_tok, d_ssm), F32),
            jax.ShapeDtypeStruct((n_tok, d_conv), BF16),
        ] + out_cast_shapes,
        scratch_shapes=[
            pltpu.VMEM((SUBLANES - 1, tm + CONV_HALO - SUBLANES, CONV_COLS), F32),
            pltpu.VMEM((tm, d_conv), F32),
        ] + [pltpu.VMEM((tm + CONV_HALO, CONV_COLS), F32)] * (d_conv // CONV_COLS),
        compiler_params=pltpu.CompilerParams(
            dimension_semantics=("arbitrary",), vmem_limit_bytes=VMEM_LIMIT),
        name="in_proj",
    )(x2, gain, w_in, conv_w, ln_g, ln_b, *mats)


def _ssm_params(log_dt, lam_re, lam_im, b_re, b_im, c_re, c_im, d):
    n_groups = lam_re.shape[0]
    nb = n_groups // GROUPS_PER_BLOCK
    half = STATE_W // 2
    lam = lax.complex(lam_re.astype(F32), lam_im.astype(F32))
    lam_dt = lam * jnp.exp(log_dt.astype(F32))[:, None]
    steps = jnp.arange(SSM_CHUNK + 1, dtype=F32)[:, None, None]
    pw = jnp.exp(lam_dt[None] * steps)
    b_bar = ((pw[1] - 1.0) / lam)[..., None] * lax.complex(b_re.astype(F32), b_im.astype(F32))
    b_t = b_bar.reshape(nb, GROUPS_PER_BLOCK, SSM_STATE, SSM_GROUP).transpose(0, 3, 1, 2)
    b_t = b_t.reshape(nb, SSM_GROUP, half)
    c_t = lax.complex(c_re.astype(F32), c_im.astype(F32))
    c_t = c_t.reshape(nb, GROUPS_PER_BLOCK, SSM_GROUP, SSM_STATE).transpose(0, 2, 1, 3)
    c_t = c_t.reshape(nb, SSM_GROUP, half)
    bc = jnp.stack([b_t.real, b_t.imag, c_t.real, c_t.imag], axis=1)
    pw_t = pw.reshape(SSM_CHUNK + 1, nb, half).transpose(1, 0, 2)
    pw_t = jnp.stack([pw_t.real, pw_t.imag], axis=1)
    drow = d.astype(F32).reshape(nb, 1, LANES)

    row = jnp.arange(SUBLANES)
    expo = jnp.stack([jnp.where(row >= 1, 1, 0), jnp.where(row >= 2, 2, 0),
                      jnp.where(row >= 4, 4, 0), row + 1])
    keep = jnp.stack([row >= 1, row >= 2, row >= 4, row >= 0])
    ap = jnp.exp(lam_dt[None, None] * (SSM_CHUNK * expo.astype(F32))[:, :, None, None])
    ap = jnp.where(keep[:, :, None, None], ap, 0.0)
    ap = jnp.stack([ap.real, ap.imag], axis=2)
    ap = ap.reshape(4, SUBLANES, 2, nb, half)
    atab = ap.transpose(3, 0, 1, 2, 4).reshape(nb, 4, SUBLANES, STATE_W)
    return pw_t, bc, drow, atab


def _toeplitz_rows(k):
    return (k // (MXU_COLS // LANES) + 1) * MXU_COLS


def _ssm_kernel(u_ref, pw_ref, bc_ref, d_ref, at_ref, y_ref, w1_scr, qct_scr, b_scr, s_scr,
                ucv_scr):
    cw = SSM_CHUNK * LANES
    half = STATE_W // 2
    n_chunks = u_ref.shape[0] // SSM_CHUNK

    @pl.when(pl.program_id(1) == 0)
    def _build_operators():
        row_g = lax.broadcasted_iota(jnp.int32, (LANES, STATE_W), 0) // SSM_GROUP
        col_g = (lax.broadcasted_iota(jnp.int32, (LANES, STATE_W), 1) % half) // SSM_STATE
        smask = row_g == col_g
        b_r, b_i, c_r, c_i = bc_ref[0], bc_ref[1], bc_ref[2], bc_ref[3]

        def slab(re, im):
            v = jnp.concatenate([jnp.concatenate([re, im], axis=1)] * GROUPS_PER_BLOCK, axis=0)
            return jnp.where(smask, v, 0.0).astype(BF16)

        for m in range(SSM_CHUNK):
            rows = slice(m * LANES, (m + 1) * LANES)
            e = SSM_CHUNK - 1 - m
            p_r, p_i = pw_ref[0, e:e + 1, :], pw_ref[1, e:e + 1, :]
            w1_scr[rows, cw:] = slab(p_r * b_r - p_i * b_i, p_r * b_i + p_i * b_r)
            q_r, q_i = pw_ref[0, m + 1:m + 2, :], pw_ref[1, m + 1:m + 2, :]
            qct_scr[rows, :] = slab(c_r * q_r - c_i * q_i, -(c_r * q_i + c_i * q_r))

        c0 = slab(c_r, -c_i)
        diag = (lax.broadcasted_iota(jnp.int32, (LANES, LANES), 0)
                == lax.broadcasted_iota(jnp.int32, (LANES, LANES), 1))
        blocks = []
        for m in range(SSM_CHUNK):
            e = SSM_CHUNK - 1 - m
            tap = lax.dot_general(w1_scr[e * LANES:(e + 1) * LANES, cw:], c0,
                                  (((1,), (1,)), ((), ())), preferred_element_type=F32)
            if m == 0:
                tap = tap + jnp.where(diag, d_ref[...], 0.0)
            blocks.append(tap.astype(BF16))
        zero_blk = jnp.zeros((LANES, LANES), BF16)
        for k in range(SSM_CHUNK):
            for j in range(_toeplitz_rows(k) // LANES):
                w1_scr[j * LANES:(j + 1) * LANES, k * LANES:(k + 1) * LANES] = (
                    blocks[k - j] if k >= j else zero_blk)

    per_tile = MXU_COLS // LANES
    bc = None
    for n in range(cw // MXU_COLS):
        tile = slice(n * MXU_COLS, (n + 1) * MXU_COLS)
        for tl in range(per_tile):
            t = n * per_tile + tl
            ucv_scr[:, t * LANES:(t + 1) * LANES] = (
                u_ref[pl.ds(t, n_chunks, stride=SSM_CHUNK), :].astype(BF16))
        part = jnp.dot(ucv_scr[:, tile], w1_scr[tile, cw:], preferred_element_type=F32)
        bc = part if bc is None else bc + part
    bc = pltpu.roll(bc, 1, axis=0)
    first_row = lax.broadcasted_iota(jnp.int32, bc.shape, 0) == 0
    b_scr[...] = jnp.where(first_row, 0.0, bc)

    a1, a2, a4, ap = at_ref[0], at_ref[1], at_ref[2], at_ref[3]

    def cmul_add(xr, xi, ar, ai, sr, si):
        return xr + ar * sr - ai * si, xi + ar * si + ai * sr

    def block_scan(r0, carry):
        cr, ci = carry
        x = b_scr[r0:r0 + SUBLANES, :]
        xr, xi = x[:, :half], x[:, half:]
        for d, tab in ((1, a1), (2, a2), (4, a4)):
            sr = pltpu.roll(xr, d, axis=0)
            si = pltpu.roll(xi, d, axis=0)
            xr, xi = cmul_add(xr, xi, tab[:, :half], tab[:, half:], sr, si)
        xr, xi = cmul_add(xr, xi, ap[:, :half], ap[:, half:], cr, ci)
        s_scr[r0:r0 + SUBLANES, :] = jnp.concatenate([xr, xi], axis=1)
        return xr[SUBLANES - 1:, :], xi[SUBLANES - 1:, :]

    carry = (jnp.zeros((1, half), F32), jnp.zeros((1, half), F32))
    for r0 in range(0, n_chunks, SUBLANES):
        carry = block_scan(r0, carry)

    states = s_scr[...].astype(BF16)
    for n in range(cw // MXU_COLS):
        tile = slice(n * MXU_COLS, (n + 1) * MXU_COLS)
        kdim = _toeplitz_rows(n * per_tile)
        y = jnp.dot(ucv_scr[:, 0:kdim], w1_scr[0:kdim, tile], preferred_element_type=F32)
        y = y + lax.dot_general(states, qct_scr[tile, :], (((1,), (1,)), ((), ())),
                                preferred_element_type=F32)
        y = jax.nn.gelu(y)
        for tl in range(per_tile):
            t = n * per_tile + tl
            y_ref[pl.ds(t, n_chunks, stride=SSM_CHUNK), :] = y[:, tl * LANES:(tl + 1) * LANES]


def _ssm(u, pw_t, bc, drow, atab, *, batch):
    n_tok, d_ssm = u.shape
    seq = n_tok // batch
    cps = seq // SSM_CHUNK
    cw = SSM_CHUNK * LANES
    return pl.pallas_call(
        _ssm_kernel,
        grid=(d_ssm // LANES, batch),
        in_specs=[
            pl.BlockSpec((seq, LANES), lambda b, s: (s, b)),
            pl.BlockSpec((None,) + pw_t.shape[1:], lambda b, s: (b, 0, 0, 0)),
            pl.BlockSpec((None,) + bc.shape[1:], lambda b, s: (b, 0, 0, 0)),
            pl.BlockSpec((None,) + drow.shape[1:], lambda b, s: (b, 0, 0)),
            pl.BlockSpec((None,) + atab.shape[1:], lambda b, s: (b, 0, 0, 0)),
        ],
        out_specs=pl.BlockSpec((seq, LANES), lambda b, s: (s, b)),
        out_shape=jax.ShapeDtypeStruct(u.shape, F32),
        scratch_shapes=[
            pltpu.VMEM((cw, cw + STATE_W), BF16),
            pltpu.VMEM((cw, STATE_W), BF16),
            pltpu.VMEM((cps, STATE_W), F32),
            pltpu.VMEM((cps, STATE_W), F32),
            pltpu.VMEM((cps, cw), BF16),
        ],
        compiler_params=pltpu.CompilerParams(
            dimension_semantics=("arbitrary", "arbitrary"), vmem_limit_bytes=VMEM_LIMIT),
        name="ssm",
    )(u, pw_t, bc, drow, atab)


def _mix_out_kernel(y_ref, c_ref, x_ref, wglu_ref, wout_ref, pg_ref, fg_ref, o_ref, h_ref):
    tm, d_ssm = y_ref.shape
    pr = tm // MIX_PARTS
    for part in range(MIX_PARTS):
        rows = slice(part * pr, (part + 1) * pr)
        y = y_ref[rows, :]
        gate = jnp.dot(y.astype(BF16), wglu_ref[...], preferred_element_type=F32)
        a = y * jax.nn.sigmoid(gate)

        yo = (jnp.dot(a.astype(BF16), wout_ref[0:d_ssm, :], preferred_element_type=F32)
              + jnp.dot(c_ref[rows, :], wout_ref[d_ssm:, :], preferred_element_type=F32))
        x1 = x_ref[rows, :] + _rms(yo, pg_ref[...])
        o_ref[rows, :] = x1
        h_ref[rows, :] = _rms(x1, fg_ref[...]).astype(BF16)


def _mix_out(y, c, x2, w_glu, w_out, post_g, pre_ffn_g, *, tm):
    n_tok, d_model = x2.shape
    d_ssm = y.shape[1]
    d_conv = c.shape[1]
    const = lambda i: (0, 0)
    return pl.pallas_call(
        _mix_out_kernel,
        grid=(n_tok // tm,),
        in_specs=[
            pl.BlockSpec((tm, d_ssm), lambda i: (i, 0)),
            pl.BlockSpec((tm, d_conv), lambda i: (i, 0)),
            pl.BlockSpec((tm, d_model), lambda i: (i, 0)),
            pl.BlockSpec(w_glu.shape, const),
            pl.BlockSpec(w_out.shape, const),
            pl.BlockSpec(post_g.shape, const),
            pl.BlockSpec(pre_ffn_g.shape, const),
        ],
        out_specs=[
            pl.BlockSpec((tm, d_model), lambda i: (i, 0)),
            pl.BlockSpec((tm, d_model), lambda i: (i, 0)),
        ],
        out_shape=[
            jax.ShapeDtypeStruct((n_tok, d_model), F32),
            jax.ShapeDtypeStruct((n_tok, d_model), BF16),
        ],
        compiler_params=pltpu.CompilerParams(
            dimension_semantics=("arbitrary",), vmem_limit_bytes=VMEM_LIMIT),
        name="mix_out",
    )(y, c, x2, w_glu, w_out, post_g, pre_ffn_g)


def _ffn_kernel(x_ref, h_ref, wg_ref, wv_ref, cg_ref, cv_ref, wd_ref, pg_ref, o_ref,
                acc_scr, ext_scr, carry_scr, *, tiles_per_seq):
    i = pl.program_id(0)
    j = pl.program_id(1)
    tm = x_ref.shape[0]

    @pl.when(j == 0)
    def _start_tile():
        acc_scr[...] = jnp.zeros_like(acc_scr)

    seq_start = (i % tiles_per_seq) == 0
    h = h_ref[...]

    def up_pass(w_ref, slot, cols):
        up = jnp.dot(h, w_ref[:, cols], preferred_element_type=F32)
        ext_scr[slot, 0:SUBLANES, cols] = jnp.where(seq_start, 0.0, carry_scr[j, slot, :, cols])
        ext_scr[slot, SUBLANES:, cols] = up
        carry_scr[j, slot, :, cols] = up[tm - SUBLANES:, :]

    def conv(cw_ref, slot, cols):
        out = None
        for k in range(FFN_CONV_K):
            shift = FFN_CONV_K - 1 - k
            term = cw_ref[k:k + 1, cols] * ext_scr[slot, pl.ds(SUBLANES - shift, tm), cols]
            out = term if out is None else out + term
        return out

    passes = [slice(c0, c0 + FFN_COLS) for c0 in range(0, wd_ref.shape[0], FFN_COLS)]
    for cols in passes:
        up_pass(wg_ref, 0, cols)
    for cols in passes:
        up_pass(wv_ref, 1, cols)
    for cols in passes:
        hidden = jax.nn.gelu(conv(cg_ref, 0, cols)) * conv(cv_ref, 1, cols)
        acc_scr[...] += jnp.dot(hidden.astype(BF16), wd_ref[cols, :],
                                preferred_element_type=F32)

    @pl.when(j == pl.num_programs(1) - 1)
    def _finish_tile():
        o_ref[...] = x_ref[...] + _rms(acc_scr[...], pg_ref[...])


def _ffn(x1, h2, w_up, conv_w, w_down, post_g, *, tm, tf, seq):
    n_tok, d_model = x1.shape
    d_ff = w_down.shape[0]
    n_ff = d_ff // tf
    assert w_up.shape == (2 * n_ff, d_model, tf)
    const = lambda i, j: (0, 0)
    return pl.pallas_call(
        functools.partial(_ffn_kernel, tiles_per_seq=seq // tm),
        grid=(n_tok // tm, n_ff),
        in_specs=[
            pl.BlockSpec((tm, d_model), lambda i, j: (i, 0)),
            pl.BlockSpec((tm, d_model), lambda i, j: (i, 0)),
            pl.BlockSpec((None, d_model, tf), lambda i, j: (j, 0, 0)),
            pl.BlockSpec((None, d_model, tf), lambda i, j: (j + n_ff, 0, 0)),
            pl.BlockSpec((FFN_CONV_K, tf), lambda i, j: (0, j)),
            pl.BlockSpec((FFN_CONV_K, tf), lambda i, j: (0, j + n_ff)),
            pl.BlockSpec((tf, d_model), lambda i, j: (j, 0)),
            pl.BlockSpec(post_g.shape, const),
        ],
        out_specs=pl.BlockSpec((tm, d_model), lambda i, j: (i, 0)),
        out_shape=jax.ShapeDtypeStruct((n_tok, d_model), F32),
        scratch_shapes=[
            pltpu.VMEM((tm, d_model), F32),
            pltpu.VMEM((2, tm + SUBLANES, tf), F32),
            pltpu.VMEM((n_ff, 2, SUBLANES, tf), F32),
        ],
        compiler_params=pltpu.CompilerParams(
            dimension_semantics=("arbitrary", "arbitrary"), vmem_limit_bytes=VMEM_LIMIT),
        name="ffn",
    )(x1, h2, w_up, w_up, conv_w, conv_w, w_down, post_g)


def _layer(x, pre_mix_g, w_in, log_dt, lam_re, lam_im, b_re, b_im, c_re, c_im, d, w_glu,
           conv_w, ln_g, ln_b, w_out, post_mix_g, pre_ffn_g, w_up, ffn_conv_w, w_down,
           post_ffn_g, *, tm, tf):
    batch, seq, d_model = x.shape
    d_ssm = w_glu.shape[0]
    d_conv = conv_w.shape[1]
    assert seq % tm == 0 and tm % CONV_HALO == 0 and tm % (MIX_PARTS * BF16_SUBLANES) == 0
    assert d_ssm % LANES == 0 and (seq // SSM_CHUNK) % SUBLANES == 0
    assert d_conv % CONV_COLS == 0 and tf % FFN_COLS == 0
    assert d_ssm % (LANES * (d_conv // CONV_COLS)) == 0
    row = lambda v: v.reshape(1, -1).astype(F32)

    x2 = x.reshape(batch * seq, d_model)
    u, c, w_glu16, w_out16, w_up16, w_down16 = _in_proj(
        x2, row(pre_mix_g), w_in.astype(BF16), conv_w.astype(F32), row(ln_g), row(ln_b),
        [(w_glu.astype(F32), None), (w_out.astype(F32), None), (w_up.astype(F32), tf),
         (w_down.astype(F32), None)],
        d_ssm=d_ssm, d_conv=d_conv, tm=tm, seq=seq)
    ssm_params = _ssm_params(log_dt, lam_re, lam_im, b_re, b_im, c_re, c_im, d)
    y = _ssm(u, *ssm_params, batch=batch)
    x1, h2 = _mix_out(y, c, x2, w_glu16, w_out16, row(post_mix_g), row(pre_ffn_g), tm=tm)
    out = _ffn(x1, h2, w_up16, ffn_conv_w.astype(F32), w_down16, row(post_ffn_g),
               tm=tm, tf=tf, seq=seq)
    return out.reshape(batch, seq, d_model)


def kernel(x, pre_mix_g, w_in, ssm_log_dt, ssm_lam_re, ssm_lam_im, ssm_b_re, ssm_b_im,
           ssm_c_re, ssm_c_im, ssm_d, ssm_w_glu, conv_w, conv_ln_g, conv_ln_b, w_out,
           post_mix_g, pre_ffn_g, ffn_w_up, ffn_conv_w, ffn_w_down, post_ffn_g):
    for i in range(pre_mix_g.shape[0]):
        x = _layer(x, pre_mix_g[i], w_in[i], ssm_log_dt[i], ssm_lam_re[i], ssm_lam_im[i],
                   ssm_b_re[i], ssm_b_im[i], ssm_c_re[i], ssm_c_im[i], ssm_d[i], ssm_w_glu[i],
                   conv_w[i], conv_ln_g[i], conv_ln_b[i], w_out[i], post_mix_g[i],
                   pre_ffn_g[i], ffn_w_up[i], ffn_conv_w[i], ffn_w_down[i], post_ffn_g[i],
                   tm=512, tf=512)
    return x
```

```python
import functools

import jax
import jax.numpy as jnp
from jax import lax
from jax.experimental import pallas as pl
from jax.experimental.pallas import tpu as pltpu

F32 = jnp.float32
BF16 = jnp.bfloat16

EPS = 1e-6
SSM_GROUP = 16
SSM_STATE = 64
CONV_K = 31
FFN_CONV_K = 3

LANES = 128
SUBLANES = 8
BF16_SUBLANES = 16
SSM_CHUNK = 16
GROUPS_PER_BLOCK = LANES // SSM_GROUP
STATE_W = 2 * GROUPS_PER_BLOCK * SSM_STATE
CONV_HALO = 32
MXU_COLS = 256
CONV_COLS = MXU_COLS
FFN_COLS = MXU_COLS
VMEM_LIMIT = 56 * 1024 * 1024
MIX_PARTS = 2
WEIGHT_SLOTS = 3


def _rms(x, gain):
    return x * lax.rsqrt(jnp.mean(x * x, axis=-1, keepdims=True) + EPS) * gain


def _in_proj_kernel(x_ref, g_ref, w_ref, cw_ref, lng_ref, lnb_ref, *rest,
                    d_ssm, d_conv, tiles_per_seq, n_cast):
    cast_in, rest = rest[:n_cast], rest[n_cast:]
    (u_ref, c_ref), rest = rest[:2], rest[2:]
    cast_out, rest = rest[:n_cast], rest[n_cast:]
    (zs_scr, co_scr), z_scrs = rest[:2], rest[2:]
    tm = x_ref.shape[0]
    i = pl.program_id(0)

    for src, dst in zip(cast_in, cast_out):
        if len(dst.shape) == 2:
            dst[...] = src[...].astype(BF16)
        else:
            for n in range(dst.shape[0]):
                dst[n] = src[:, n * dst.shape[2]:(n + 1) * dst.shape[2]].astype(BF16)

    @pl.when(i == 0)
    def _init_context():
        for z_scr in z_scrs:
            z_scr[...] = jnp.zeros_like(z_scr)

    seq_start = (i % tiles_per_seq) == 0
    for z_scr in z_scrs:
        z_scr[0:CONV_HALO, :] = jnp.where(seq_start, 0.0, z_scr[tm:tm + CONV_HALO, :])

    h = _rms(x_ref[...], g_ref[...]).astype(BF16)
    shifted_rows = tm + CONV_HALO - SUBLANES
    n_pass = d_conv // CONV_COLS
    u_cols = d_ssm // n_pass

    def glu_pass(cb):
        v0 = d_ssm + cb * CONV_COLS
        g0 = d_ssm + d_conv + cb * CONV_COLS
        cv = jnp.dot(h, w_ref[:, v0:v0 + CONV_COLS], preferred_element_type=F32)
        cg = jnp.dot(h, w_ref[:, g0:g0 + CONV_COLS], preferred_element_type=F32)
        z_scrs[cb][CONV_HALO:, :] = cv * jax.nn.sigmoid(cg)

    def ssm_input_pass(cb):
        cols = slice(cb * u_cols, (cb + 1) * u_cols)
        u_ref[:, cols] = jnp.dot(h, w_ref[:, cols], preferred_element_type=F32)

    def conv_pass(cb):
        cols = slice(cb * CONV_COLS, (cb + 1) * CONV_COLS)
        z_scr = z_scrs[cb]
        for r in range(1, SUBLANES):
            zs_scr[r - 1] = z_scr[pl.ds(r, shifted_rows), :]
        acc = None
        for k in range(CONV_K):
            q, r = divmod(CONV_HALO - (CONV_K - 1) + k, SUBLANES)
            if r == 0:
                src = z_scr[pl.ds(q * SUBLANES, tm), :]
            else:
                src = zs_scr[r - 1, pl.ds(q * SUBLANES, tm), :]
            term = cw_ref[k:k + 1, cols] * src
            acc = term if acc is None else acc + term
        co_scr[:, cols] = acc

    glu_pass(0)
    for cb in range(n_pass):
        if cb + 1 < n_pass:
            glu_pass(cb + 1)
        ssm_input_pass(cb)
        conv_pass(cb)

    co = co_scr[...]
    mu = jnp.mean(co, axis=-1, keepdims=True)
    xc = co - mu
    zn = xc * lax.rsqrt(jnp.mean(xc * xc, axis=-1, keepdims=True) + EPS)
    c_ref[...] = jax.nn.silu(zn * lng_ref[...] + lnb_ref[...]).astype(BF16)


def _in_proj(x2, gain, w_in, conv_w, ln_g, ln_b, to_cast, *, d_ssm, d_conv, tm, seq):
    n_tok, d_model = x2.shape
    steps = n_tok // tm
    const = lambda i: (0, 0)
    mats = [w for w, _ in to_cast]
    for w in mats:
        assert w.shape[0] % (steps * BF16_SUBLANES) == 0
    in_cast_specs = [pl.BlockSpec((w.shape[0] // steps, w.shape[1]), lambda i: (i, 0))
                     for w in mats]
    out_cast_specs, out_cast_shapes = [], []
    for w, col_tile in to_cast:
        rows, cols = w.shape
        if col_tile is None:
            out_cast_specs.append(pl.BlockSpec((rows // steps, cols), lambda i: (i, 0)))
            out_cast_shapes.append(jax.ShapeDtypeStruct((rows, cols), BF16))
        else:
            n_tiles = cols // col_tile
            out_cast_specs.append(
                pl.BlockSpec((n_tiles, rows // steps, col_tile), lambda i: (0, i, 0)))
            out_cast_shapes.append(jax.ShapeDtypeStruct((n_tiles, rows, col_tile), BF16))
    return pl.pallas_call(
        functools.partial(_in_proj_kernel, d_ssm=d_ssm, d_conv=d_conv,
                          tiles_per_seq=seq // tm, n_cast=len(to_cast)),
        grid=(steps,),
        in_specs=[
            pl.BlockSpec((tm, d_model), lambda i: (i, 0)),
            pl.BlockSpec((1, d_model), const),
            pl.BlockSpec(w_in.shape, const, pipeline_mode=pl.Buffered(1)),
            pl.BlockSpec(conv_w.shape, const),
            pl.BlockSpec(ln_g.shape, const),
            pl.BlockSpec(ln_b.shape, const),
        ] + in_cast_specs,
        out_specs=[
            pl.BlockSpec((tm, d_ssm), lambda i: (i, 0)),
            pl.BlockSpec((tm, d_conv), lambda i: (i, 0)),
        ] + out_cast_specs,
        out_shape=[
            jax.ShapeDtypeStruct((n_tok, d_ssm), F32),
            jax.ShapeDtypeStruct((n_tok, d_conv), BF16),
        ] + out_cast_shapes,
        scratch_shapes=[
            pltpu.VMEM((SUBLANES - 1, tm + CONV_HALO - SUBLANES, CONV_COLS), F32),
            pltpu.VMEM((tm, d_conv), F32),
        ] + [pltpu.VMEM((tm + CONV_HALO, CONV_COLS), F32)] * (d_conv // CONV_COLS),
        compiler_params=pltpu.CompilerParams(
            dimension_semantics=("arbitrary",), vmem_limit_bytes=VMEM_LIMIT),
        name="in_proj",
    )(x2, gain, w_in, conv_w, ln_g, ln_b, *mats)


def _ssm_params(log_dt, lam_re, lam_im, b_re, b_im, c_re, c_im, d):
    n_groups = lam_re.shape[0]
    nb = n_groups // GROUPS_PER_BLOCK
    half = STATE_W // 2
    lam = lax.complex(lam_re.astype(F32), lam_im.astype(F32))
    lam_dt = lam * jnp.exp(log_dt.astype(F32))[:, None]
    steps = jnp.arange(SSM_CHUNK + 1, dtype=F32)[:, None, None]
    pw = jnp.exp(lam_dt[None] * steps)
    b_bar = ((pw[1] - 1.0) / lam)[..., None] * lax.complex(b_re.astype(F32), b_im.astype(F32))
    b_t = b_bar.reshape(nb, GROUPS_PER_BLOCK, SSM_STATE, SSM_GROUP).transpose(0, 3, 1, 2)
    b_t = b_t.reshape(nb, SSM_GROUP, half)
    c_t = lax.complex(c_re.astype(F32), c_im.astype(F32))
    c_t = c_t.reshape(nb, GROUPS_PER_BLOCK, SSM_GROUP, SSM_STATE).transpose(0, 2, 1, 3)
    c_t = c_t.reshape(nb, SSM_GROUP, half)
    bc = jnp.stack([b_t.real, b_t.imag, c_t.real, c_t.imag], axis=1)
    pw_t = pw.reshape(SSM_CHUNK + 1, nb, half).transpose(1, 0, 2)
    pw_t = jnp.stack([pw_t.real, pw_t.imag], axis=1)
    drow = d.astype(F32).reshape(nb, 1, LANES)

    row = jnp.arange(SUBLANES)
    expo = jnp.stack([jnp.where(row >= 1, 1, 0), jnp.where(row >= 2, 2, 0),
                      jnp.where(row >= 4, 4, 0), row + 1])
    keep = jnp.stack([row >= 1, row >= 2, row >= 4, row >= 0])
    ap = jnp.exp(lam_dt[None, None] * (SSM_CHUNK * expo.astype(F32))[:, :, None, None])
    ap = jnp.where(keep[:, :, None, None], ap, 0.0)
    ap = jnp.stack([ap.real, ap.imag], axis=2)
    ap = ap.reshape(4, SUBLANES, 2, nb, half)
    atab = ap.transpose(3, 0, 1, 2, 4).reshape(nb, 4, SUBLANES, STATE_W)
    return pw_t, bc, drow, atab


def _toeplitz_rows(k):
    return (k // (MXU_COLS // LANES) + 1) * MXU_COLS


def _ssm_kernel(u_ref, pw_ref, bc_ref, d_ref, at_ref, y_ref, w1_scr, qct_scr, b_scr, s_scr,
                ucv_scr):
    cw = SSM_CHUNK * LANES
    half = STATE_W // 2
    n_chunks = u_ref.shape[0] // SSM_CHUNK

    @pl.when(pl.program_id(1) == 0)
    def _build_operators():
        row_g = lax.broadcasted_iota(jnp.int32, (LANES, STATE_W), 0) // SSM_GROUP
        col_g = (lax.broadcasted_iota(jnp.int32, (LANES, STATE_W), 1) % half) // SSM_STATE
        smask = row_g == col_g
        b_r, b_i, c_r, c_i = bc_ref[0], bc_ref[1], bc_ref[2], bc_ref[3]

        def slab(re, im):
            v = jnp.concatenate([jnp.concatenate([re, im], axis=1)] * GROUPS_PER_BLOCK, axis=0)
            return jnp.where(smask, v, 0.0).astype(BF16)

        for m in range(SSM_CHUNK):
            rows = slice(m * LANES, (m + 1) * LANES)
            e = SSM_CHUNK - 1 - m
            p_r, p_i = pw_ref[0, e:e + 1, :], pw_ref[1, e:e + 1, :]
            w1_scr[rows, cw:] = slab(p_r * b_r - p_i * b_i, p_r * b_i + p_i * b_r)
            q_r, q_i = pw_ref[0, m + 1:m + 2, :], pw_ref[1, m + 1:m + 2, :]
            qct_scr[rows, :] = slab(c_r * q_r - c_i * q_i, -(c_r * q_i + c_i * q_r))

        c0 = slab(c_r, -c_i)
        diag = (lax.broadcasted_iota(jnp.int32, (LANES, LANES), 0)
                == lax.broadcasted_iota(jnp.int32, (LANES, LANES), 1))
        blocks = []
        for m in range(SSM_CHUNK):
            e = SSM_CHUNK - 1 - m
            tap = lax.dot_general(w1_scr[e * LANES:(e + 1) * LANES, cw:], c0,
                                  (((1,), (1,)), ((), ())), preferred_element_type=F32)
            if m == 0:
                tap = tap + jnp.where(diag, d_ref[...], 0.0)
            blocks.append(tap.astype(BF16))
        zero_blk = jnp.zeros((LANES, LANES), BF16)
        for k in range(SSM_CHUNK):
            for j in range(_toeplitz_rows(k) // LANES):
                w1_scr[j * LANES:(j + 1) * LANES, k * LANES:(k + 1) * LANES] = (
                    blocks[k - j] if k >= j else zero_blk)

    per_tile = MXU_COLS // LANES
    bc = None
    for n in range(cw // MXU_COLS):
        tile = slice(n * MXU_COLS, (n + 1) * MXU_COLS)
        for tl in range(per_tile):
            t = n * per_tile + tl
            ucv_scr[:, t * LANES:(t + 1) * LANES] = (
                u_ref[pl.ds(t, n_chunks, stride=SSM_CHUNK), :].astype(BF16))
        part = jnp.dot(ucv_scr[:, tile], w1_scr[tile, cw:], preferred_element_type=F32)
        bc = part if bc is None else bc + part
    bc = pltpu.roll(bc, 1, axis=0)
    first_row = lax.broadcasted_iota(jnp.int32, bc.shape, 0) == 0
    b_scr[...] = jnp.where(first_row, 0.0, bc)

    a1, a2, a4, ap = at_ref[0], at_ref[1], at_ref[2], at_ref[3]

    def cmul_add(xr, xi, ar, ai, sr, si):
        return xr + ar * sr - ai * si, xi + ar * si + ai * sr

    def block_scan(r0, carry):
        cr, ci = carry
        x = b_scr[r0:r0 + SUBLANES, :]
        xr, xi = x[:, :half], x[:, half:]
        for d, tab in ((1, a1), (2, a2), (4, a4)):
            sr = pltpu.roll(xr, d, axis=0)
            si = pltpu.roll(xi, d, axis=0)
            xr, xi = cmul_add(xr, xi, tab[:, :half], tab[:, half:], sr, si)
        xr, xi = cmul_add(xr, xi, ap[:, :half], ap[:, half:], cr, ci)
        s_scr[r0:r0 + SUBLANES, :] = jnp.concatenate([xr, xi], axis=1)
        return xr[SUBLANES - 1:, :], xi[SUBLANES - 1:, :]

    carry = (jnp.zeros((1, half), F32), jnp.zeros((1, half), F32))
    for r0 in range(0, n_chunks, SUBLANES):
        carry = block_scan(r0, carry)

    states = s_scr[...].astype(BF16)
    for n in range(cw // MXU_COLS):
        tile = slice(n * MXU_COLS, (n + 1) * MXU_COLS)
        kdim = _toeplitz_rows(n * per_tile)
        y = jnp.dot(ucv_scr[:, 0:kdim], w1_scr[0:kdim, tile], preferred_element_type=F32)
        y = y + lax.dot_general(states, qct_scr[tile, :], (((1,), (1,)), ((), ())),
                                preferred_element_type=F32)
        y = jax.nn.gelu(y)
        for tl in range(per_tile):
            t = n * per_tile + tl
            y_ref[pl.ds(t, n_chunks, stride=SSM_CHUNK), :] = y[:, tl * LANES:(tl + 1) * LANES]


def _ssm(u, pw_t, bc, drow, atab, *, batch):
    n_tok, d_ssm = u.shape
    seq = n_tok // batch
    cps = seq // SSM_CHUNK
    cw = SSM_CHUNK * LANES
    return pl.pallas_call(
        _ssm_kernel,
        grid=(d_ssm // LANES, batch),
        in_specs=[
            pl.BlockSpec((seq, LANES), lambda b, s: (s, b)),
            pl.BlockSpec((None,) + pw_t.shape[1:], lambda b, s: (b, 0, 0, 0)),
            pl.BlockSpec((None,) + bc.shape[1:], lambda b, s: (b, 0, 0, 0)),
            pl.BlockSpec((None,) + drow.shape[1:], lambda b, s: (b, 0, 0)),
            pl.BlockSpec((None,) + atab.shape[1:], lambda b, s: (b, 0, 0, 0)),
        ],
        out_specs=pl.BlockSpec((seq, LANES), lambda b, s: (s, b)),
        out_shape=jax.ShapeDtypeStruct(u.shape, F32),
        scratch_shapes=[
            pltpu.VMEM((cw, cw + STATE_W), BF16),
            pltpu.VMEM((cw, STATE_W), BF16),
            pltpu.VMEM((cps, STATE_W), F32),
            pltpu.VMEM((cps, STATE_W), F32),
            pltpu.VMEM((cps, cw), BF16),
        ],
        compiler_params=pltpu.CompilerParams(
            dimension_semantics=("arbitrary", "arbitrary"), vmem_limit_bytes=VMEM_LIMIT),
        name="ssm",
    )(u, pw_t, bc, drow, atab)


def _mix_out_kernel(y_ref, c_ref, x_ref, wglu_ref, wout_ref, pg_ref, fg_ref, o_ref, h_ref):
    tm, d_ssm = y_ref.shape
    pr = tm // MIX_PARTS
    for part in range(MIX_PARTS):
        rows = slice(part * pr, (part + 1) * pr)
        y = y_ref[rows, :]
        gate = jnp.dot(y.astype(BF16), wglu_ref[...], preferred_element_type=F32)
        a = y * jax.nn.sigmoid(gate)

        yo = (jnp.dot(a.astype(BF16), wout_ref[0:d_ssm, :], preferred_element_type=F32)
              + jnp.dot(c_ref[rows, :], wout_ref[d_ssm:, :], preferred_element_type=F32))
        x1 = x_ref[rows, :] + _rms(yo, pg_ref[...])
        o_ref[rows, :] = x1
        h_ref[rows, :] = _rms(x1, fg_ref[...]).astype(BF16)


def _mix_out(y, c, x2, w_glu, w_out, post_g, pre_ffn_g, *, tm):
    n_tok, d_model = x2.shape
    d_ssm = y.shape[1]
    d_conv = c.shape[1]
    const = lambda i: (0, 0)
    return pl.pallas_call(
        _mix_out_kernel,
        grid=(n_tok // tm,),
        in_specs=[
            pl.BlockSpec((tm, d_ssm), lambda i: (i, 0)),
            pl.BlockSpec((tm, d_conv), lambda i: (i, 0)),
            pl.BlockSpec((tm, d_model), lambda i: (i, 0)),
            pl.BlockSpec(w_glu.shape, const),
            pl.BlockSpec(w_out.shape, const),
            pl.BlockSpec(post_g.shape, const),
            pl.BlockSpec(pre_ffn_g.shape, const),
        ],
        out_specs=[
            pl.BlockSpec((tm, d_model), lambda i: (i, 0)),
            pl.BlockSpec((tm, d_model), lambda i: (i, 0)),
        ],
        out_shape=[
            jax.ShapeDtypeStruct((n_tok, d_model), F32),
            jax.ShapeDtypeStruct((n_tok, d_model), BF16),
        ],
        compiler_params=pltpu.CompilerParams(
            dimension_semantics=("arbitrary",), vmem_limit_bytes=VMEM_LIMIT),
        name="mix_out",
    )(y, c, x2, w_glu, w_out, post_g, pre_ffn_g)


def _ffn_kernel(x_ref, h_ref, cg_ref, cv_ref, pg_ref, wup_hbm, wdn_hbm, o_ref,
                acc_scr, ext_scr, carry_scr, wup_buf, wdn_buf, w_sem, *, tiles_per_seq, n_ff):
    i = pl.program_id(0)
    j = pl.program_id(1)
    tm = x_ref.shape[0]
    tf = wdn_buf.shape[1]
    n_steps = pl.num_programs(0) * n_ff
    step = i * n_ff + j

    def weight_copies(s, slot):
        t = lax.rem(s, n_ff)
        rows = pl.ds(pl.multiple_of(t * tf, tf), tf)
        return (pltpu.make_async_copy(wup_hbm.at[t], wup_buf.at[slot, 0], w_sem.at[slot, 0]),
                pltpu.make_async_copy(wup_hbm.at[t + n_ff], wup_buf.at[slot, 1],
                                      w_sem.at[slot, 1]),
                pltpu.make_async_copy(wdn_hbm.at[rows], wdn_buf.at[slot], w_sem.at[slot, 2]))

    @pl.when(step == 0)
    def _prime_ring():
        for s in range(WEIGHT_SLOTS - 1):
            for copy in weight_copies(s, s):
                copy.start()

    slot = lax.rem(step, WEIGHT_SLOTS)
    for copy in weight_copies(step, slot):
        copy.wait()

    @pl.when(step + (WEIGHT_SLOTS - 1) < n_steps)
    def _prefetch():
        ahead = step + (WEIGHT_SLOTS - 1)
        for copy in weight_copies(ahead, lax.rem(ahead, WEIGHT_SLOTS)):
            copy.start()

    wg_ref = wup_buf.at[slot, 0]
    wv_ref = wup_buf.at[slot, 1]
    wd_ref = wdn_buf.at[slot]

    @pl.when(j == 0)
    def _start_tile():
        acc_scr[...] = jnp.zeros_like(acc_scr)

    seq_start = (i % tiles_per_seq) == 0
    h = h_ref[...]

    def up_pass(w_ref, slot, cols):
        up = jnp.dot(h, w_ref[:, cols], preferred_element_type=F32)
        ext_scr[slot, 0:SUBLANES, cols] = jnp.where(seq_start, 0.0, carry_scr[j, slot, :, cols])
        ext_scr[slot, SUBLANES:, cols] = up
        carry_scr[j, slot, :, cols] = up[tm - SUBLANES:, :]

    def conv(cw_ref, slot, cols):
        out = None
        for k in range(FFN_CONV_K):
            shift = FFN_CONV_K - 1 - k
            term = cw_ref[k:k + 1, cols] * ext_scr[slot, pl.ds(SUBLANES - shift, tm), cols]
            out = term if out is None else out + term
        return out

    passes = [slice(c0, c0 + FFN_COLS) for c0 in range(0, wd_ref.shape[0], FFN_COLS)]
    for cols in passes:
        up_pass(wg_ref, 0, cols)
    for cols in passes:
        up_pass(wv_ref, 1, cols)
    for cols in passes:
        hidden = jax.nn.gelu(conv(cg_ref, 0, cols)) * conv(cv_ref, 1, cols)
        acc_scr[...] += jnp.dot(hidden.astype(BF16), wd_ref[cols, :],
                                preferred_element_type=F32)

    @pl.when(j == pl.num_programs(1) - 1)
    def _finish_tile():
        o_ref[...] = x_ref[...] + _rms(acc_scr[...], pg_ref[...])


def _ffn(x1, h2, w_up, conv_w, w_down, post_g, *, tm, tf, seq):
    n_tok, d_model = x1.shape
    d_ff = w_down.shape[0]
    n_ff = d_ff // tf
    assert w_up.shape == (2 * n_ff, d_model, tf)
    const = lambda i, j: (0, 0)
    return pl.pallas_call(
        functools.partial(_ffn_kernel, tiles_per_seq=seq // tm, n_ff=n_ff),
        grid=(n_tok // tm, n_ff),
        in_specs=[
            pl.BlockSpec((tm, d_model), lambda i, j: (i, 0)),
            pl.BlockSpec((tm, d_model), lambda i, j: (i, 0)),
            pl.BlockSpec((FFN_CONV_K, tf), lambda i, j: (0, j)),
            pl.BlockSpec((FFN_CONV_K, tf), lambda i, j: (0, j + n_ff)),
            pl.BlockSpec(post_g.shape, const),
            pl.BlockSpec(memory_space=pl.ANY),
            pl.BlockSpec(memory_space=pl.ANY),
        ],
        out_specs=pl.BlockSpec((tm, d_model), lambda i, j: (i, 0)),
        out_shape=jax.ShapeDtypeStruct((n_tok, d_model), F32),
        scratch_shapes=[
            pltpu.VMEM((tm, d_model), F32),
            pltpu.VMEM((2, tm + SUBLANES, tf), F32),
            pltpu.VMEM((n_ff, 2, SUBLANES, tf), F32),
            pltpu.VMEM((WEIGHT_SLOTS, 2, d_model, tf), BF16),
            pltpu.VMEM((WEIGHT_SLOTS, tf, d_model), BF16),
            pltpu.SemaphoreType.DMA((WEIGHT_SLOTS, 3)),
        ],
        compiler_params=pltpu.CompilerParams(
            dimension_semantics=("arbitrary", "arbitrary"), vmem_limit_bytes=VMEM_LIMIT),
        name="ffn",
    )(x1, h2, conv_w, conv_w, post_g, w_up, w_down)


def _layer(x, pre_mix_g, w_in, log_dt, lam_re, lam_im, b_re, b_im, c_re, c_im, d, w_glu,
           conv_w, ln_g, ln_b, w_out, post_mix_g, pre_ffn_g, w_up, ffn_conv_w, w_down,
           post_ffn_g, *, tm, tf):
    batch, seq, d_model = x.shape
    d_ssm = w_glu.shape[0]
    d_conv = conv_w.shape[1]
    assert seq % tm == 0 and tm % CONV_HALO == 0 and tm % (MIX_PARTS * BF16_SUBLANES) == 0
    assert d_ssm % LANES == 0 and (seq // SSM_CHUNK) % SUBLANES == 0
    assert d_conv % CONV_COLS == 0 and tf % FFN_COLS == 0
    assert d_ssm % (LANES * (d_conv // CONV_COLS)) == 0
    row = lambda v: v.reshape(1, -1).astype(F32)

    x2 = x.reshape(batch * seq, d_model)
    u, c, w_glu16, w_out16, w_up16, w_down16 = _in_proj(
        x2, row(pre_mix_g), w_in.astype(BF16), conv_w.astype(F32), row(ln_g), row(ln_b),
        [(w_glu.astype(F32), None), (w_out.astype(F32), None), (w_up.astype(F32), tf),
         (w_down.astype(F32), None)],
        d_ssm=d_ssm, d_conv=d_conv, tm=tm, seq=seq)
    ssm_params = _ssm_params(log_dt, lam_re, lam_im, b_re, b_im, c_re, c_im, d)
    y = _ssm(u, *ssm_params, batch=batch)
    x1, h2 = _mix_out(y, c, x2, w_glu16, w_out16, row(post_mix_g), row(pre_ffn_g), tm=tm)
    out = _ffn(x1, h2, w_up16, ffn_conv_w.astype(F32), w_down16, row(post_ffn_g),
               tm=tm, tf=tf, seq=seq)
    return out.reshape(batch, seq, d_model)


def kernel(x, pre_mix_g, w_in, ssm_log_dt, ssm_lam_re, ssm_lam_im, ssm_b_re, ssm_b_im,
           ssm_c_re, ssm_c_im, ssm_d, ssm_w_glu, conv_w, conv_ln_g, conv_ln_b, w_out,
           post_mix_g, pre_ffn_g, ffn_w_up, ffn_conv_w, ffn_w_down, post_ffn_g):
    for i in range(pre_mix_g.shape[0]):
        x = _layer(x, pre_mix_g[i], w_in[i], ssm_log_dt[i], ssm_lam_re[i], ssm_lam_im[i],
                   ssm_b_re[i], ssm_b_im[i], ssm_c_re[i], ssm_c_im[i], ssm_d[i], ssm_w_glu[i],
                   conv_w[i], conv_ln_g[i], conv_ln_b[i], w_out[i], post_mix_g[i],
                   pre_ffn_g[i], ffn_w_up[i], ffn_conv_w[i], ffn_w_down[i], post_ffn_g[i],
                   tm=512, tf=512)
    return x
```
